```python
import jax, jax.numpy as jnp
from jax import lax
import numpy as np

D_MODEL = 1024
BATCH = 8
SEQ = 4096
DEPTH = 2

N_MEM = 256
HEAD_DIM = 64
MIX_HEADS = 12
MIX_WIDTH = MIX_HEADS * HEAD_DIM
MEM_HEADS = 4
MEM_WIDTH = MEM_HEADS * HEAD_DIM
CAT_WIDTH = MIX_WIDTH + MEM_WIDTH
D_FF = 2816
CONV_WIDTH = 4
LRU_C = 8.0
BLOCK_Q = 128
NORM_EPS = 1e-6
F32 = jnp.float32

kernel_name = "hybrid_rglru_fox_macaron_memxattn"


def rmsnorm(x, g):
    xf = x.astype(F32)
    y = xf * lax.rsqrt(jnp.mean(xf * xf, axis=-1, keepdims=True) + NORM_EPS)
    return (y * g.astype(F32)).astype(x.dtype)


def swiglu(x, w_in, w_out):
    gate, up = jnp.split(x @ w_in, 2, axis=-1)
    return (jax.nn.silu(gate) * up) @ w_out


def memory_keys_values(mem, norm_g, w_kv, k_norm_g):
    B, N, _ = mem.shape
    k, v = jnp.split(rmsnorm(mem, norm_g) @ w_kv, 2, axis=-1)
    k = rmsnorm(k.reshape(B, N, MEM_HEADS, HEAD_DIM), k_norm_g)
    v = v.reshape(B, N, MEM_HEADS, HEAD_DIM)
    return k, v


def memory_cross_attention(q, mk, mv):
    B, S = q.shape[:2]
    s = jnp.einsum('bshd,bnhd->bhsn', q, mk, preferred_element_type=F32) * (HEAD_DIM ** -0.5)
    p = jax.nn.softmax(s, axis=-1)
    o = jnp.einsum('bhsn,bnhd->bshd', p.astype(mv.dtype), mv)
    return o.reshape(B, S, MEM_WIDTH)


def causal_depthwise_conv(u, w, b):
    S = u.shape[1]
    up = jnp.pad(u, ((0, 0), (CONV_WIDTH - 1, 0), (0, 0)))
    y = b
    for tap in range(CONV_WIDTH):
        y = y + up[:, tap:tap + S] * w[tap]
    return y


def _linear_recurrence_combine(c1, c2):
    a1, b1 = c1
    a2, b2 = c2
    return a1 * a2, a2 * b1 + b2


def rg_lru(xc, w_rg, b_rg, w_ig, b_ig, lam):
    B, S, W = xc.shape
    xh = xc.reshape(B, S, MIX_HEADS, HEAD_DIM)
    r = jax.nn.sigmoid(jnp.einsum('bshi,hij->bshj', xh, w_rg).reshape(B, S, W) + b_rg).astype(F32)
    gi = jax.nn.sigmoid(jnp.einsum('bshi,hij->bshj', xh, w_ig).reshape(B, S, W) + b_ig).astype(F32)
    log_a = -LRU_C * r * jax.nn.softplus(-lam.astype(F32))
    a = jnp.exp(log_a)
    bx = jnp.sqrt(-jnp.expm1(2.0 * log_a)) * (gi * xc.astype(F32))
    _, hs = lax.associative_scan(_linear_recurrence_combine, (a, bx), axis=1)
    return hs.astype(xc.dtype)


def forgetting_attention(q, k, v, f_logit, b_f, q_g, k_g):
    B, S, _ = q.shape
    q = rmsnorm(q.reshape(B, S, MIX_HEADS, HEAD_DIM), q_g).transpose(0, 2, 1, 3)
    k = rmsnorm(k.reshape(B, S, MIX_HEADS, HEAD_DIM), k_g).transpose(0, 2, 1, 3)
    v = v.reshape(B, S, MIX_HEADS, HEAD_DIM).transpose(0, 2, 1, 3)
    log_f = jax.nn.log_sigmoid(f_logit.astype(F32) + b_f.astype(F32))
    cum = jnp.cumsum(log_f, axis=1).transpose(0, 2, 1)
    nb = S // BLOCK_Q
    q_blocks = q.reshape(B, MIX_HEADS, nb, BLOCK_Q, HEAD_DIM).transpose(2, 0, 1, 3, 4)
    c_blocks = cum.reshape(B, MIX_HEADS, nb, BLOCK_Q).transpose(2, 0, 1, 3)
    key_pos = jnp.arange(S)
    scale = HEAD_DIM ** -0.5

    def one_block(args):
        qb, cb, bi = args
        s = jnp.einsum('bhqd,bhkd->bhqk', qb, k, preferred_element_type=F32) * scale
        s = s + cb[..., None] - cum[:, :, None, :]
        q_pos = bi * BLOCK_Q + jnp.arange(BLOCK_Q)
        s = jnp.where(key_pos[None, :] <= q_pos[:, None], s, -jnp.inf)
        p = jax.nn.softmax(s, axis=-1)
        return jnp.einsum('bhqk,bhkd->bhqd', p.astype(v.dtype), v)

    o = lax.map(one_block, (q_blocks, c_blocks, jnp.arange(nb)))
    return o.transpose(1, 0, 3, 2, 4).reshape(B, S, MIX_WIDTH)


def setup_inputs(seed: int = 0) -> dict:
    key = jax.random.key(seed)
    ks = iter(jax.random.split(key, 40))
    n_lru = (DEPTH + 1) // 2
    n_fox = DEPTH // 2

    def nrm(shape, fan_in):
        return jax.random.normal(next(ks), shape, F32) * (fan_in ** -0.5)

    def gain(shape):
        return 1.0 + 0.1 * jax.random.normal(next(ks), shape, F32)

    def small(shape, s=0.1):
        return s * jax.random.normal(next(ks), shape, F32)

    x = jax.random.normal(next(ks), (BATCH, SEQ, D_MODEL), F32)
    mem = jax.random.normal(next(ks), (BATCH, N_MEM, D_MODEL), F32)
    u = jax.random.uniform(next(ks), (n_lru, MIX_WIDTH), F32, minval=0.9, maxval=0.999)
    a0 = u ** (1.0 / LRU_C)
    lru_lambda = jnp.log(a0) - jnp.log1p(-a0)
    return {
        "x": x,
        "mem": mem,
        "mem_norm_g": gain((D_MODEL,)),
        "mem_w_kv": nrm((D_MODEL, 2 * MEM_WIDTH), D_MODEL),
        "mem_k_norm_g": gain((HEAD_DIM,)),
        "ffn1_norm_g": gain((DEPTH, D_MODEL)),
        "ffn1_w_in": nrm((DEPTH, D_MODEL, 2 * D_FF), D_MODEL),
        "ffn1_w_out": nrm((DEPTH, D_FF, D_MODEL), D_FF),
        "mix_norm_g": gain((DEPTH, D_MODEL)),
        "mix_w_out": nrm((DEPTH, CAT_WIDTH, D_MODEL), CAT_WIDTH),
        "memq_norm_g": gain((DEPTH, HEAD_DIM)),
        "ffn2_norm_g": gain((DEPTH, D_MODEL)),
        "ffn2_w_in": nrm((DEPTH, D_MODEL, 2 * D_FF), D_MODEL),
        "ffn2_w_out": nrm((DEPTH, D_FF, D_MODEL), D_FF),
        "lru_w_in": nrm((n_lru, D_MODEL, 2 * MIX_WIDTH + MEM_WIDTH), D_MODEL),
        "lru_conv_w": nrm((n_lru, CONV_WIDTH, MIX_WIDTH), CONV_WIDTH),
        "lru_conv_b": small((n_lru, MIX_WIDTH), 0.01),
        "lru_w_rg": nrm((n_lru, MIX_HEADS, HEAD_DIM, HEAD_DIM), HEAD_DIM),
        "lru_b_rg": small((n_lru, MIX_WIDTH)),
        "lru_w_ig": nrm((n_lru, MIX_HEADS, HEAD_DIM, HEAD_DIM), HEAD_DIM),
        "lru_b_ig": small((n_lru, MIX_WIDTH)),
        "lru_lambda": lru_lambda,
        "fox_w_in": nrm((n_fox, D_MODEL, 3 * MIX_WIDTH + MIX_HEADS + MEM_WIDTH), D_MODEL),
        "fox_b_f": jax.random.uniform(next(ks), (n_fox, MIX_HEADS), F32, minval=1.0, maxval=6.0),
        "fox_q_norm_g": gain((n_fox, HEAD_DIM)),
        "fox_k_norm_g": gain((n_fox, HEAD_DIM)),
    }


def reference(x, mem, mem_norm_g, mem_w_kv, mem_k_norm_g,
              ffn1_norm_g, ffn1_w_in, ffn1_w_out,
              mix_norm_g, mix_w_out, memq_norm_g,
              ffn2_norm_g, ffn2_w_in, ffn2_w_out,
              lru_w_in, lru_conv_w, lru_conv_b, lru_w_rg, lru_b_rg, lru_w_ig, lru_b_ig, lru_lambda,
              fox_w_in, fox_b_f, fox_q_norm_g, fox_k_norm_g):
    B, S, _ = x.shape
    mem_k, mem_v = memory_keys_values(mem, mem_norm_g, mem_w_kv, mem_k_norm_g)
    h = x
    for i in range(DEPTH):
        j = i // 2
        h = h + 0.5 * swiglu(rmsnorm(h, ffn1_norm_g[i]), ffn1_w_in[i], ffn1_w_out[i])
        hn = rmsnorm(h, mix_norm_g[i])
        if i % 2 == 0:
            x_br, g_br, q_mem = jnp.split(hn @ lru_w_in[j], [MIX_WIDTH, 2 * MIX_WIDTH], axis=-1)
            xc = causal_depthwise_conv(x_br, lru_conv_w[j], lru_conv_b[j])
            tok = rg_lru(xc, lru_w_rg[j], lru_b_rg[j], lru_w_ig[j], lru_b_ig[j], lru_lambda[j])
            tok = tok * jax.nn.gelu(g_br)
        else:
            q, k, v, f_logit, q_mem = jnp.split(
                hn @ fox_w_in[j],
                [MIX_WIDTH, 2 * MIX_WIDTH, 3 * MIX_WIDTH, 3 * MIX_WIDTH + MIX_HEADS], axis=-1)
            tok = forgetting_attention(q, k, v, f_logit, fox_b_f[j], fox_q_norm_g[j], fox_k_norm_g[j])
        q_mem = rmsnorm(q_mem.reshape(B, S, MEM_HEADS, HEAD_DIM), memq_norm_g[i])
        cross = memory_cross_attention(q_mem, mem_k, mem_v)
        h = h + jnp.concatenate([tok, cross], axis=-1) @ mix_w_out[i]
        h = h + 0.5 * swiglu(rmsnorm(h, ffn2_norm_g[i]), ffn2_w_in[i], ffn2_w_out[i])
    return h
```

```python
import functools

import jax
import jax.numpy as jnp
from jax import lax
from jax.experimental import pallas as pl
from jax.experimental.pallas import tpu as pltpu

F32 = jnp.float32
BF16 = jnp.bfloat16

D_MODEL = 1024
HEAD_DIM = 64
MIX_HEADS = 12
MIX_WIDTH = MIX_HEADS * HEAD_DIM
MEM_HEADS = 4
MEM_WIDTH = MEM_HEADS * HEAD_DIM
N_MEM = 256
D_FF = 2816
CONV_WIDTH = 4
LRU_C = 8.0
NORM_EPS = 1e-6
SCORE_SCALE = HEAD_DIM ** -0.5

LANES = 128
SUBLANES = 8
MXU_DIM = 256
HEADS_PER_MXU_TILE = MXU_DIM // HEAD_DIM
F_PAD = LANES
FOX_COLS = 3 * MIX_WIDTH + MEM_WIDTH + F_PAD
CUM_ROWS = 16
VMEM_LIMIT_BYTES = 56 * 1024 * 1024

ROW_TILE = 256
ATTN_TILE = 512


def _dot(a, b):
    return jnp.dot(a, b, preferred_element_type=F32)


def _dot_nt(a, b):
    return lax.dot_general(a, b, (((1,), (1,)), ((), ())), preferred_element_type=F32)


def _rms(x, g):
    ms = jnp.mean(x * x, axis=-1, keepdims=True)
    return x * lax.rsqrt(ms + NORM_EPS) * g


def _head_rms(x, g, bd_ref):
    outs = []
    for c in range(x.shape[1] // MXU_DIM):
        xc = x[:, c * MXU_DIM:(c + 1) * MXU_DIM]
        x2 = xc * xc
        hi = x2.astype(BF16)
        lo = (x2 - hi.astype(F32)).astype(BF16)
        ms = (_dot(hi, bd_ref[...]) + _dot(lo, bd_ref[...])) * (1.0 / HEAD_DIM)
        outs.append(xc * lax.rsqrt(ms + NORM_EPS))
    y = outs[0] if len(outs) == 1 else jnp.concatenate(outs, axis=-1)
    return y * g


def _softplus(x):
    return jnp.maximum(x, 0.0) + jnp.log1p(jnp.exp(-jnp.abs(x)))


def _memkv_kernel(mem_ref, g_ref, w_ref, kg_ref, bd_ref, kbd_ref, vbd_ref):
    xn = _rms(mem_ref[0], g_ref[...]).astype(BF16)
    kv = _dot(xn, w_ref[...])
    k = _head_rms(kv[:, :MEM_WIDTH], kg_ref[...], bd_ref)
    v = kv[:, MEM_WIDTH:]
    kt = k.T
    shape = (MEM_WIDTH, N_MEM)
    row_head = lax.shift_right_logical(lax.broadcasted_iota(jnp.int32, shape, 0), 6)
    col_head = lax.shift_right_logical(lax.broadcasted_iota(jnp.int32, (N_MEM, MEM_WIDTH), 1), 6)
    for h in range(MEM_HEADS):
        kbd_ref[0, :, h * N_MEM:(h + 1) * N_MEM] = jnp.where(row_head == h, kt, 0.0).astype(BF16)
        vbd_ref[0, h * N_MEM:(h + 1) * N_MEM, :] = jnp.where(col_head == h, v, 0.0).astype(BF16)


def _memory_kv(mem, g, w_kv, kg, bd):
    nb = mem.shape[0]
    const = lambda shape: pl.BlockSpec(shape, lambda b: (0,) * len(shape))
    return pl.pallas_call(
        _memkv_kernel,
        grid=(nb,),
        in_specs=[
            pl.BlockSpec((1, N_MEM, D_MODEL), lambda b: (b, 0, 0)),
            const((1, D_MODEL)),
            const((D_MODEL, 2 * MEM_WIDTH)),
            const((1, MEM_WIDTH)),
            const((MXU_DIM, MXU_DIM)),
        ],
        out_specs=[
            pl.BlockSpec((1, MEM_WIDTH, MEM_HEADS * N_MEM), lambda b: (b, 0, 0)),
            pl.BlockSpec((1, MEM_HEADS * N_MEM, MEM_WIDTH), lambda b: (b, 0, 0)),
        ],
        out_shape=[
            jax.ShapeDtypeStruct((nb, MEM_WIDTH, MEM_HEADS * N_MEM), BF16),
            jax.ShapeDtypeStruct((nb, MEM_HEADS * N_MEM, MEM_WIDTH), BF16),
        ],
        compiler_params=pltpu.CompilerParams(
            dimension_semantics=("arbitrary",), vmem_limit_bytes=VMEM_LIMIT_BYTES),
        name="memory_kv",
    )(mem, g, w_kv, kg, bd)


def _ffn_kernel(h_ref, g_ref, win_ref, wout_ref, o_ref):
    x = h_ref[...]
    xn = _rms(x, g_ref[...]).astype(BF16)
    gu = _dot(xn, win_ref[...])
    gate = gu[:, :D_FF]
    up = gu[:, D_FF:]
    act = (jax.nn.silu(gate) * up).astype(BF16)
    o_ref[...] = x + 0.5 * _dot(act, wout_ref[...])


def _ffn(h, g, w_in, w_out, tm):
    t = h.shape[0]
    resident = lambda shape: pl.BlockSpec(shape, lambda i: (0, 0), pipeline_mode=pl.Buffered(1))
    return pl.pallas_call(
        _ffn_kernel,
        grid=(t // tm,),
        in_specs=[
            pl.BlockSpec((tm, D_MODEL), lambda i: (i, 0)),
            resident((1, D_MODEL)),
            resident((D_MODEL, 2 * D_FF)),
            resident((D_FF, D_MODEL)),
        ],
        out_specs=pl.BlockSpec((tm, D_MODEL), lambda i: (i, 0)),
        out_shape=jax.ShapeDtypeStruct((t, D_MODEL), F32),
        compiler_params=pltpu.CompilerParams(
            dimension_semantics=("arbitrary",), vmem_limit_bytes=VMEM_LIMIT_BYTES),
        name="swiglu_ffn",
    )(h, g, w_in, w_out)


def _cross_attention(qm, kbd_ref, vbd_ref):
    s = _dot(qm, kbd_ref[0])
    probs = []
    for h in range(MEM_HEADS):
        sh = s[:, h * N_MEM:(h + 1) * N_MEM]
        e = jnp.exp(sh - jnp.max(sh, axis=-1, keepdims=True))
        inv = 1.0 / jnp.sum(e, axis=-1, keepdims=True)
        probs.append((e * inv).astype(BF16))
    return _dot(jnp.concatenate(probs, axis=-1), vbd_ref[0])


def _mix_out(h, tok, cross, wout_ref):
    cat = jnp.concatenate([tok.astype(BF16), cross.astype(BF16)], axis=-1)
    return h + _dot(cat, wout_ref[...])


def _lru_kernel(h_ref, g_ref, win_ref, cw_ref, cb_ref, wg_ref, brg_ref, big_ref, lam_ref,
                mqg_ref, bd_ref, kbd_ref, vbd_ref, wout_ref, o_ref,
                xs_ref, a_ref, b_ref, hprev_ref, *, ts):
    @pl.when(pl.program_id(1) == 0)
    def _():
        xs_ref[0:SUBLANES, :] = jnp.zeros((SUBLANES, MIX_WIDTH), F32)
        hprev_ref[...] = jnp.zeros((SUBLANES, MIX_WIDTH), F32)

    h = h_ref[0]
    hn = _rms(h, g_ref[...]).astype(BF16)
    proj = _dot(hn, win_ref[...])
    g_br = proj[:, MIX_WIDTH:2 * MIX_WIDTH]
    q_mem = proj[:, 2 * MIX_WIDTH:]

    xs_ref[SUBLANES:SUBLANES + ts, :] = proj[:, :MIX_WIDTH]
    xc = cb_ref[...]
    for tap in range(CONV_WIDTH):
        off = SUBLANES - (CONV_WIDTH - 1) + tap
        xc = xc + xs_ref[off:off + ts, :] * cw_ref[tap:tap + 1, :]
    xs_ref[0:SUBLANES, :] = xs_ref[ts:ts + SUBLANES, :]

    xcb = xc.astype(BF16)
    r_parts, i_parts = [], []
    for c in range(MIX_WIDTH // MXU_DIM):
        gg = _dot(xcb[:, c * MXU_DIM:(c + 1) * MXU_DIM], wg_ref[c])
        r_parts.append(gg[:, :MXU_DIM])
        i_parts.append(gg[:, MXU_DIM:])
    r = jax.nn.sigmoid(jnp.concatenate(r_parts, axis=-1) + brg_ref[...])
    gi = jax.nn.sigmoid(jnp.concatenate(i_parts, axis=-1) + big_ref[...])
    log_a = -LRU_C * r * _softplus(-lam_ref[...])
    t = jnp.tanh(-log_a)
    a_ref[...] = jnp.exp(log_a)
    b_ref[...] = jnp.sqrt(2.0 * t / (1.0 + t)) * (gi * xc)

    row = lax.broadcasted_iota(jnp.int32, (SUBLANES, MIX_WIDTH), 0)

    def scan_block(i, hprev):
        start = pl.multiple_of(i * SUBLANES, SUBLANES)
        a = a_ref[pl.ds(start, SUBLANES), :]
        b = b_ref[pl.ds(start, SUBLANES), :]
        for k in (1, 2, 4):
            keep = row >= k
            a_sh = jnp.where(keep, pltpu.roll(a, k, 0), 1.0)
            b_sh = jnp.where(keep, pltpu.roll(b, k, 0), 0.0)
            b = b + a * b_sh
            a = a * a_sh
        hs = b + a * hprev
        b_ref[pl.ds(start, SUBLANES), :] = hs
        return jnp.broadcast_to(hs[SUBLANES - 1:SUBLANES, :], (SUBLANES, MIX_WIDTH))

    hprev_ref[...] = lax.fori_loop(0, ts // SUBLANES, scan_block, hprev_ref[...], unroll=4)

    tok = b_ref[...] * jax.nn.gelu(g_br)
    qm = (_head_rms(q_mem, mqg_ref[...], bd_ref) * SCORE_SCALE).astype(BF16)
    cross = _cross_attention(qm, kbd_ref, vbd_ref)
    o_ref[0] = _mix_out(h, tok, cross, wout_ref)


def _lru_mix(h, g, w_in, conv_w, conv_b, wg, b_rg, b_ig, lam, mqg, bd, kbd, vbd, w_out, ts):
    nb, s, _ = h.shape
    const = lambda shape: pl.BlockSpec(shape, lambda b, i: (0,) * len(shape), pipeline_mode=pl.Buffered(1))
    per_batch = lambda shape: pl.BlockSpec((1,) + shape, lambda b, i: (b, 0, 0))
    return pl.pallas_call(
        functools.partial(_lru_kernel, ts=ts),
        grid=(nb, s // ts),
        in_specs=[
            pl.BlockSpec((1, ts, D_MODEL), lambda b, i: (b, i, 0)),
            const((1, D_MODEL)),
            const((D_MODEL, 2 * MIX_WIDTH + MEM_WIDTH)),
            const((CONV_WIDTH, MIX_WIDTH)),
            const((1, MIX_WIDTH)),
            const((MIX_WIDTH // MXU_DIM, MXU_DIM, 2 * MXU_DIM)),
            const((1, MIX_WIDTH)),
            const((1, MIX_WIDTH)),
            const((1, MIX_WIDTH)),
            const((1, MEM_WIDTH)),
            const((MXU_DIM, MXU_DIM)),
            per_batch((MEM_WIDTH, MEM_HEADS * N_MEM)),
            per_batch((MEM_HEADS * N_MEM, MEM_WIDTH)),
            const((D_MODEL, D_MODEL)),
        ],
        out_specs=pl.BlockSpec((1, ts, D_MODEL), lambda b, i: (b, i, 0)),
        out_shape=jax.ShapeDtypeStruct(h.shape, F32),
        scratch_shapes=[
            pltpu.VMEM((ts + SUBLANES, MIX_WIDTH), F32),
            pltpu.VMEM((ts, MIX_WIDTH), F32),
            pltpu.VMEM((ts, MIX_WIDTH), F32),
            pltpu.VMEM((SUBLANES, MIX_WIDTH), F32),
        ],
        compiler_params=pltpu.CompilerParams(
            dimension_semantics=("arbitrary", "arbitrary"), vmem_limit_bytes=VMEM_LIMIT_BYTES),
        name="rglru_mix",
    )(h, g, w_in, conv_w, conv_b, wg, b_rg, b_ig, lam, mqg, bd, kbd, vbd, w_out)


def _fox_proj_kernel(h_ref, g_ref, w_ref, qg_ref, kg_ref, mqg_ref, bf_ref, bd_ref, tri_ref,
                     q_ref, k_ref, v_ref, qm_ref, cum_ref, carry_ref):
    @pl.when(pl.program_id(1) == 0)
    def _():
        carry_ref[...] = jnp.zeros((1, F_PAD), F32)

    hn = _rms(h_ref[0], g_ref[...]).astype(BF16)
    proj = _dot(hn, w_ref[...])
    w = MIX_WIDTH
    q_ref[0] = (_head_rms(proj[:, :w], qg_ref[...], bd_ref) * SCORE_SCALE).astype(BF16)
    k_ref[0] = _head_rms(proj[:, w:2 * w], kg_ref[...], bd_ref).astype(BF16)
    v_ref[0] = proj[:, 2 * w:3 * w].astype(BF16)
    qm = proj[:, 3 * w:3 * w + MEM_WIDTH]
    qm_ref[0] = (_head_rms(qm, mqg_ref[...], bd_ref) * SCORE_SCALE).astype(BF16)

    log_f = -_softplus(-(proj[:, 3 * w + MEM_WIDTH:] + bf_ref[...]))
    p0 = log_f.astype(BF16)
    r1 = log_f - p0.astype(F32)
    p1 = r1.astype(BF16)
    p2 = (r1 - p1.astype(F32)).astype(BF16)
    tri = tri_ref[...]
    cum = carry_ref[...] + (_dot(tri, p0) + _dot(tri, p1) + _dot(tri, p2))
    carry_ref[...] = cum[cum.shape[0] - 1:, :]
    cum_ref[0] = cum.T[:CUM_ROWS, :]


def _fox_proj(h, g, w, qg, kg, mqg, bf, bd, tri, ts):
    nb, s, _ = h.shape
    const = lambda shape: pl.BlockSpec(shape, lambda b, i: (0,) * len(shape), pipeline_mode=pl.Buffered(1))
    tile = lambda width: pl.BlockSpec((1, ts, width), lambda b, i: (b, i, 0))
    return pl.pallas_call(
        _fox_proj_kernel,
        grid=(nb, s // ts),
        in_specs=[
            tile(D_MODEL),
            const((1, D_MODEL)),
            const((D_MODEL, FOX_COLS)),
            const((1, MIX_WIDTH)),
            const((1, MIX_WIDTH)),
            const((1, MEM_WIDTH)),
            const((1, F_PAD)),
            const((MXU_DIM, MXU_DIM)),
            const((ts, ts)),
        ],
        out_specs=[
            tile(MIX_WIDTH), tile(MIX_WIDTH), tile(MIX_WIDTH), tile(MEM_WIDTH),
            pl.BlockSpec((1, CUM_ROWS, ts), lambda b, i: (b, 0, i)),
        ],
        out_shape=[
            jax.ShapeDtypeStruct((nb, s, MIX_WIDTH), BF16),
            jax.ShapeDtypeStruct((nb, s, MIX_WIDTH), BF16),
            jax.ShapeDtypeStruct((nb, s, MIX_WIDTH), BF16),
            jax.ShapeDtypeStruct((nb, s, MEM_WIDTH), BF16),
            jax.ShapeDtypeStruct((nb, CUM_ROWS, s), F32),
        ],
        scratch_shapes=[pltpu.VMEM((1, F_PAD), F32)],
        compiler_params=pltpu.CompilerParams(
            dimension_semantics=("arbitrary", "arbitrary"), vmem_limit_bytes=VMEM_LIMIT_BYTES),
        name="fox_proj",
    )(h, g, w, qg, kg, mqg, bf, bd, tri)


def _fox_attn_kernel(q_ref, k_ref, v_ref, cum_ref, o_ref, qh_ref, m_ref, l_ref, acc_ref, *, tq):
    hp = pl.program_id(1)
    qi = pl.program_id(2)
    lane = lax.broadcasted_iota(jnp.int32, (tq, LANES), 1)
    qf = q_ref[0].astype(F32)
    qh_ref[0] = jnp.where(lane < HEAD_DIM, qf, 0.0).astype(BF16)
    qh_ref[1] = jnp.where(lane >= HEAD_DIM, qf, 0.0).astype(BF16)
    m_ref[...] = jnp.full(m_ref.shape, -jnp.inf, F32)
    l_ref[...] = jnp.zeros(l_ref.shape, F32)
    acc_ref[...] = jnp.zeros(acc_ref.shape, F32)
    q_start = pl.multiple_of(qi * tq, tq)

    def step(j, masked):
        k_start = pl.multiple_of(j * tq, tq)
        kt = k_ref[0, pl.ds(k_start, tq), :]
        vt = v_ref[0, pl.ds(k_start, tq), :]
        for a in range(2):
            c_q = cum_ref[0, pl.ds(2 * hp + a, 1), pl.ds(q_start, tq)]
            c_k = cum_ref[0, pl.ds(2 * hp + a, 1), pl.ds(k_start, tq)]
            s = _dot_nt(qh_ref[a], kt) + (c_q[:, 0:1] - c_k)
            if masked:
                rows = lax.broadcasted_iota(jnp.int32, (tq, tq), 0)
                cols = lax.broadcasted_iota(jnp.int32, (tq, tq), 1)
                s = jnp.where(cols <= rows, s, -jnp.inf)
            m_prev = m_ref[a]
            m_new = jnp.maximum(m_prev, jnp.max(s, axis=-1, keepdims=True))
            alpha = jnp.exp(m_prev - m_new)
            p = jnp.exp(s - m_new)
            l_ref[a] = alpha * l_ref[a] + jnp.sum(p, axis=-1, keepdims=True)
            acc_ref[a] = alpha * acc_ref[a] + _dot(p.astype(BF16), vt)
            m_ref[a] = m_new

    def body(j, carry):
        step(j, False)
        return carry

    lax.fori_loop(0, qi, body, 0)
    step(qi, True)
    out0 = acc_ref[0] * (1.0 / l_ref[0])
    out1 = acc_ref[1] * (1.0 / l_ref[1])
    o_ref[0] = jnp.where(lane < HEAD_DIM, out0, out1).astype(BF16)


def _fox_attention(q, k, v, cum, tq):
    nb, s, _ = q.shape
    return pl.pallas_call(
        functools.partial(_fox_attn_kernel, tq=tq),
        grid=(nb, MIX_WIDTH // LANES, s // tq),
        in_specs=[
            pl.BlockSpec((1, tq, LANES), lambda b, p, i: (b, i, p)),
            pl.BlockSpec((1, s, LANES), lambda b, p, i: (b, 0, p)),
            pl.BlockSpec((1, s, LANES), lambda b, p, i: (b, 0, p)),
            pl.BlockSpec((1, CUM_ROWS, s), lambda b, p, i: (b, 0, 0)),
        ],
        out_specs=pl.BlockSpec((1, tq, LANES), lambda b, p, i: (b, i, p)),
        out_shape=jax.ShapeDtypeStruct((nb, s, MIX_WIDTH), BF16),
        scratch_shapes=[
            pltpu.VMEM((2, tq, LANES), BF16),
            pltpu.VMEM((2, tq, 1), F32),
            pltpu.VMEM((2, tq, 1), F32),
            pltpu.VMEM((2, tq, LANES), F32),
        ],
        compiler_params=pltpu.CompilerParams(
            dimension_semantics=("arbitrary", "arbitrary", "arbitrary"),
            vmem_limit_bytes=VMEM_LIMIT_BYTES),
        name="fox_attention",
    )(q, k, v, cum)


def _fox_out_kernel(h_ref, tok_ref, qm_ref, kbd_ref, vbd_ref, wout_ref, o_ref):
    cross = _cross_attention(qm_ref[0], kbd_ref, vbd_ref)
    o_ref[0] = _mix_out(h_ref[0], tok_ref[0], cross, wout_ref)


def _fox_out(h, tok, qm, kbd, vbd, w_out, ts):
    nb, s, _ = h.shape
    tile = lambda width: pl.BlockSpec((1, ts, width), lambda b, i: (b, i, 0))
    per_batch = lambda shape: pl.BlockSpec((1,) + shape, lambda b, i: (b, 0, 0))
    return pl.pallas_call(
        _fox_out_kernel,
        grid=(nb, s // ts),
        in_specs=[
            tile(D_MODEL), tile(MIX_WIDTH), tile(MEM_WIDTH),
            per_batch((MEM_WIDTH, MEM_HEADS * N_MEM)),
            per_batch((MEM_HEADS * N_MEM, MEM_WIDTH)),
            pl.BlockSpec((D_MODEL, D_MODEL), lambda b, i: (0, 0), pipeline_mode=pl.Buffered(1)),
        ],
        out_specs=tile(D_MODEL),
        out_shape=jax.ShapeDtypeStruct(h.shape, F32),
        compiler_params=pltpu.CompilerParams(
            dimension_semantics=("arbitrary", "arbitrary"), vmem_limit_bytes=VMEM_LIMIT_BYTES),
        name="fox_out",
    )(h, tok, qm, kbd, vbd, w_out)


def _row(v):
    return v.reshape(1, -1).astype(F32)


def _tile_heads(g, heads):
    return jnp.tile(g.astype(F32), heads).reshape(1, -1)


def _gate_weights(w_rg, w_ig):
    def block_diag(w):
        groups = w.reshape(MIX_HEADS // HEADS_PER_MXU_TILE, HEADS_PER_MXU_TILE, HEAD_DIM, HEAD_DIM)
        eye = jnp.eye(HEADS_PER_MXU_TILE, dtype=w.dtype)
        bd = jnp.einsum('ghij,hk->ghikj', groups, eye)
        return bd.reshape(-1, MXU_DIM, MXU_DIM)
    return jnp.concatenate([block_diag(w_rg), block_diag(w_ig)], axis=-1).astype(BF16)


def _fox_weights(w):
    qkv = w[:, :3 * MIX_WIDTH]
    f = w[:, 3 * MIX_WIDTH:3 * MIX_WIDTH + MIX_HEADS]
    qm = w[:, 3 * MIX_WIDTH + MIX_HEADS:]
    f = jnp.pad(f, ((0, 0), (0, F_PAD - MIX_HEADS)))
    return jnp.concatenate([qkv, qm, f], axis=-1).astype(BF16)


def kernel(x, mem, mem_norm_g, mem_w_kv, mem_k_norm_g, ffn1_norm_g, ffn1_w_in, ffn1_w_out, mix_norm_g, mix_w_out, memq_norm_g, ffn2_norm_g, ffn2_w_in, ffn2_w_out, lru_w_in, lru_conv_w, lru_conv_b, lru_w_rg, lru_b_rg, lru_w_ig, lru_b_ig, lru_lambda, fox_w_in, fox_b_f, fox_q_norm_g, fox_k_norm_g):
    nb, s, d = x.shape
    assert d == D_MODEL and mem.shape[1:] == (N_MEM, D_MODEL)
    ts = min(ROW_TILE, s)
    tq = min(ATTN_TILE, s)
    assert s % ts == 0 and s % tq == 0 and (nb * s) % ts == 0

    head_id = jnp.arange(MXU_DIM) // HEAD_DIM
    bd = (head_id[:, None] == head_id[None, :]).astype(BF16)
    tri = (jnp.arange(ts)[:, None] >= jnp.arange(ts)[None, :]).astype(BF16)

    kbd, vbd = _memory_kv(mem, _row(mem_norm_g), mem_w_kv.astype(BF16),
                          _tile_heads(mem_k_norm_g, MEM_HEADS), bd)

    def ffn(h, g, w_in, w_out):
        out = _ffn(h.reshape(nb * s, d), _row(g), w_in.astype(BF16), w_out.astype(BF16), ts)
        return out.reshape(nb, s, d)

    h = x
    h = ffn(h, ffn1_norm_g[0], ffn1_w_in[0], ffn1_w_out[0])
    h = _lru_mix(h, _row(mix_norm_g[0]), lru_w_in[0].astype(BF16), lru_conv_w[0].astype(F32),
                 _row(lru_conv_b[0]), _gate_weights(lru_w_rg[0], lru_w_ig[0]), _row(lru_b_rg[0]),
                 _row(lru_b_ig[0]), _row(lru_lambda[0]), _tile_heads(memq_norm_g[0], MEM_HEADS),
                 bd, kbd, vbd, mix_w_out[0].astype(BF16), ts)
    h = ffn(h, ffn2_norm_g[0], ffn2_w_in[0], ffn2_w_out[0])
    h = ffn(h, ffn1_norm_g[1], ffn1_w_in[1], ffn1_w_out[1])
    bf = jnp.pad(fox_b_f[0].astype(F32), (0, F_PAD - MIX_HEADS)).reshape(1, F_PAD)
    q, k, v, qm, cum = _fox_proj(h, _row(mix_norm_g[1]), _fox_weights(fox_w_in[0]),
                                 _tile_heads(fox_q_norm_g[0], MIX_HEADS),
                                 _tile_heads(fox_k_norm_g[0], MIX_HEADS),
                                 _tile_heads(memq_norm_g[1], MEM_HEADS), bf, bd, tri, ts)
    tok = _fox_attention(q, k, v, cum, tq)
    h = _fox_out(h, tok, qm, kbd, vbd, mix_w_out[1].astype(BF16), ts)
    h = ffn(h, ffn2_norm_g[1], ffn2_w_in[1], ffn2_w_out[1])
    return h
```

```python
import functools

import jax
import jax.numpy as jnp
from jax import lax
from jax.experimental import pallas as pl
from jax.experimental.pallas import tpu as pltpu

F32 = jnp.float32
BF16 = jnp.bfloat16

D_MODEL = 1024
HEAD_DIM = 64
MIX_HEADS = 12
MIX_WIDTH = MIX_HEADS * HEAD_DIM
MEM_HEADS = 4
MEM_WIDTH = MEM_HEADS * HEAD_DIM
N_MEM = 256
D_FF = 2816
CONV_WIDTH = 4
LRU_C = 8.0
NORM_EPS = 1e-6
SCORE_SCALE = HEAD_DIM ** -0.5

LANES = 128
SUBLANES = 8
MXU_DIM = 256
HEADS_PER_MXU_TILE = MXU_DIM // HEAD_DIM
F_PAD = LANES
FOX_COLS = 3 * MIX_WIDTH + MEM_WIDTH + F_PAD
E_STRIDE = 16
VMEM_LIMIT_BYTES = 56 * 1024 * 1024

ROW_TILE = 256
ATTN_TILE = 512


def _dot(a, b):
    return jnp.dot(a, b, preferred_element_type=F32)


def _dot_nt(a, b):
    return lax.dot_general(a, b, (((1,), (1,)), ((), ())), preferred_element_type=F32)


def _rms(x, g):
    ms = jnp.mean(x * x, axis=-1, keepdims=True)
    return x * lax.rsqrt(ms + NORM_EPS) * g


def _head_rms(x, g, bd_ref):
    outs = []
    for c in range(x.shape[1] // MXU_DIM):
        xc = x[:, c * MXU_DIM:(c + 1) * MXU_DIM]
        x2 = xc * xc
        hi = x2.astype(BF16)
        lo = (x2 - hi.astype(F32)).astype(BF16)
        ms = (_dot(hi, bd_ref[...]) + _dot(lo, bd_ref[...])) * (1.0 / HEAD_DIM)
        outs.append(xc * lax.rsqrt(ms + NORM_EPS))
    y = outs[0] if len(outs) == 1 else jnp.concatenate(outs, axis=-1)
    return y * g


def _softplus(x):
    return jnp.maximum(x, 0.0) + jnp.log1p(jnp.exp(-jnp.abs(x)))


def _memkv_kernel(mem_ref, g_ref, w_ref, kg_ref, bd_ref, kbd_ref, vbd_ref):
    xn = _rms(mem_ref[0], g_ref[...]).astype(BF16)
    kv = _dot(xn, w_ref[...])
    k = _head_rms(kv[:, :MEM_WIDTH], kg_ref[...], bd_ref)
    v = kv[:, MEM_WIDTH:]
    kt = k.T
    shape = (MEM_WIDTH, N_MEM)
    row_head = lax.shift_right_logical(lax.broadcasted_iota(jnp.int32, shape, 0), 6)
    col_head = lax.shift_right_logical(lax.broadcasted_iota(jnp.int32, (N_MEM, MEM_WIDTH), 1), 6)
    for h in range(MEM_HEADS):
        kbd_ref[0, :, h * N_MEM:(h + 1) * N_MEM] = jnp.where(row_head == h, kt, 0.0).astype(BF16)
        vbd_ref[0, h * N_MEM:(h + 1) * N_MEM, :] = jnp.where(col_head == h, v, 0.0).astype(BF16)


def _memory_kv(mem, g, w_kv, kg, bd):
    nb = mem.shape[0]
    const = lambda shape: pl.BlockSpec(shape, lambda b: (0,) * len(shape))
    return pl.pallas_call(
        _memkv_kernel,
        grid=(nb,),
        in_specs=[
            pl.BlockSpec((1, N_MEM, D_MODEL), lambda b: (b, 0, 0)),
            const((1, D_MODEL)),
            const((D_MODEL, 2 * MEM_WIDTH)),
            const((1, MEM_WIDTH)),
            const((MXU_DIM, MXU_DIM)),
        ],
        out_specs=[
            pl.BlockSpec((1, MEM_WIDTH, MEM_HEADS * N_MEM), lambda b: (b, 0, 0)),
            pl.BlockSpec((1, MEM_HEADS * N_MEM, MEM_WIDTH), lambda b: (b, 0, 0)),
        ],
        out_shape=[
            jax.ShapeDtypeStruct((nb, MEM_WIDTH, MEM_HEADS * N_MEM), BF16),
            jax.ShapeDtypeStruct((nb, MEM_HEADS * N_MEM, MEM_WIDTH), BF16),
        ],
        compiler_params=pltpu.CompilerParams(
            dimension_semantics=("arbitrary",), vmem_limit_bytes=VMEM_LIMIT_BYTES),
        name="memory_kv",
    )(mem, g, w_kv, kg, bd)


def _ffn_kernel(h_ref, g_ref, win_ref, wout_ref, o_ref):
    x = h_ref[...]
    xn = _rms(x, g_ref[...]).astype(BF16)
    gu = _dot(xn, win_ref[...])
    gate = gu[:, :D_FF]
    up = gu[:, D_FF:]
    act = (jax.nn.silu(gate) * up).astype(BF16)
    o_ref[...] = x + 0.5 * _dot(act, wout_ref[...])


def _ffn(h, g, w_in, w_out, tm):
    t = h.shape[0]
    resident = lambda shape: pl.BlockSpec(shape, lambda i: (0, 0), pipeline_mode=pl.Buffered(1))
    return pl.pallas_call(
        _ffn_kernel,
        grid=(t // tm,),
        in_specs=[
            pl.BlockSpec((tm, D_MODEL), lambda i: (i, 0)),
            resident((1, D_MODEL)),
            resident((D_MODEL, 2 * D_FF)),
            resident((D_FF, D_MODEL)),
        ],
        out_specs=pl.BlockSpec((tm, D_MODEL), lambda i: (i, 0)),
        out_shape=jax.ShapeDtypeStruct((t, D_MODEL), F32),
        compiler_params=pltpu.CompilerParams(
            dimension_semantics=("arbitrary",), vmem_limit_bytes=VMEM_LIMIT_BYTES),
        name="swiglu_ffn",
    )(h, g, w_in, w_out)


def _cross_attention(qm, kbd_ref, vbd_ref):
    s = _dot(qm, kbd_ref[0])
    probs = []
    for h in range(MEM_HEADS):
        sh = s[:, h * N_MEM:(h + 1) * N_MEM]
        e = jnp.exp(sh - jnp.max(sh, axis=-1, keepdims=True))
        inv = 1.0 / jnp.sum(e, axis=-1, keepdims=True)
        probs.append((e * inv).astype(BF16))
    return _dot(jnp.concatenate(probs, axis=-1), vbd_ref[0])


def _mix_out(h, tok, cross, wout_ref):
    cat = jnp.concatenate([tok.astype(BF16), cross.astype(BF16)], axis=-1)
    return h + _dot(cat, wout_ref[...])


def _lru_kernel(h_ref, g_ref, win_ref, cw_ref, cb_ref, wg_ref, brg_ref, big_ref, lam_ref,
                mqg_ref, bd_ref, kbd_ref, vbd_ref, wout_ref, o_ref,
                xs_ref, a_ref, b_ref, hprev_ref, *, ts):
    @pl.when(pl.program_id(1) == 0)
    def _():
        xs_ref[0:SUBLANES, :] = jnp.zeros((SUBLANES, MIX_WIDTH), F32)
        hprev_ref[...] = jnp.zeros((SUBLANES, MIX_WIDTH), F32)

    h = h_ref[0]
    hn = _rms(h, g_ref[...]).astype(BF16)
    proj = _dot(hn, win_ref[...])
    g_br = proj[:, MIX_WIDTH:2 * MIX_WIDTH]
    q_mem = proj[:, 2 * MIX_WIDTH:]

    xs_ref[SUBLANES:SUBLANES + ts, :] = proj[:, :MIX_WIDTH]
    xc = cb_ref[...]
    for tap in range(CONV_WIDTH):
        off = SUBLANES - (CONV_WIDTH - 1) + tap
        xc = xc + xs_ref[off:off + ts, :] * cw_ref[tap:tap + 1, :]
    xs_ref[0:SUBLANES, :] = xs_ref[ts:ts + SUBLANES, :]

    xcb = xc.astype(BF16)
    r_parts, i_parts = [], []
    for c in range(MIX_WIDTH // MXU_DIM):
        gg = _dot(xcb[:, c * MXU_DIM:(c + 1) * MXU_DIM], wg_ref[c])
        r_parts.append(gg[:, :MXU_DIM])
        i_parts.append(gg[:, MXU_DIM:])
    r = jax.nn.sigmoid(jnp.concatenate(r_parts, axis=-1) + brg_ref[...])
    gi = jax.nn.sigmoid(jnp.concatenate(i_parts, axis=-1) + big_ref[...])
    log_a = -LRU_C * r * _softplus(-lam_ref[...])
    t = jnp.tanh(-log_a)
    a_ref[...] = jnp.exp(log_a)
    b_ref[...] = jnp.sqrt(2.0 * t / (1.0 + t)) * (gi * xc)

    row = lax.broadcasted_iota(jnp.int32, (SUBLANES, MIX_WIDTH), 0)

    def scan_block(i, hprev):
        start = pl.multiple_of(i * SUBLANES, SUBLANES)
        a = a_ref[pl.ds(start, SUBLANES), :]
        b = b_ref[pl.ds(start, SUBLANES), :]
        for k in (1, 2, 4):
            keep = row >= k
            a_sh = jnp.where(keep, pltpu.roll(a, k, 0), 1.0)
            b_sh = jnp.where(keep, pltpu.roll(b, k, 0), 0.0)
            b = b + a * b_sh
            a = a * a_sh
        hs = b + a * hprev
        b_ref[pl.ds(start, SUBLANES), :] = hs
        return jnp.broadcast_to(hs[SUBLANES - 1:SUBLANES, :], (SUBLANES, MIX_WIDTH))

    hprev_ref[...] = lax.fori_loop(0, ts // SUBLANES, scan_block, hprev_ref[...], unroll=4)

    tok = b_ref[...] * jax.nn.gelu(g_br)
    qm = (_head_rms(q_mem, mqg_ref[...], bd_ref) * SCORE_SCALE).astype(BF16)
    cross = _cross_attention(qm, kbd_ref, vbd_ref)
    o_ref[0] = _mix_out(h, tok, cross, wout_ref)


def _lru_mix(h, g, w_in, conv_w, conv_b, wg, b_rg, b_ig, lam, mqg, bd, kbd, vbd, w_out, ts):
    nb, s, _ = h.shape
    const = lambda shape: pl.BlockSpec(shape, lambda b, i: (0,) * len(shape), pipeline_mode=pl.Buffered(1))
    per_batch = lambda shape: pl.BlockSpec((1,) + shape, lambda b, i: (b, 0, 0))
    return pl.pallas_call(
        functools.partial(_lru_kernel, ts=ts),
        grid=(nb, s // ts),
        in_specs=[
            pl.BlockSpec((1, ts, D_MODEL), lambda b, i: (b, i, 0)),
            const((1, D_MODEL)),
            const((D_MODEL, 2 * MIX_WIDTH + MEM_WIDTH)),
            const((CONV_WIDTH, MIX_WIDTH)),
            const((1, MIX_WIDTH)),
            const((MIX_WIDTH // MXU_DIM, MXU_DIM, 2 * MXU_DIM)),
            const((1, MIX_WIDTH)),
            const((1, MIX_WIDTH)),
            const((1, MIX_WIDTH)),
            const((1, MEM_WIDTH)),
            const((MXU_DIM, MXU_DIM)),
            per_batch((MEM_WIDTH, MEM_HEADS * N_MEM)),
            per_batch((MEM_HEADS * N_MEM, MEM_WIDTH)),
            const((D_MODEL, D_MODEL)),
        ],
        out_specs=pl.BlockSpec((1, ts, D_MODEL), lambda b, i: (b, i, 0)),
        out_shape=jax.ShapeDtypeStruct(h.shape, F32),
        scratch_shapes=[
            pltpu.VMEM((ts + SUBLANES, MIX_WIDTH), F32),
            pltpu.VMEM((ts, MIX_WIDTH), F32),
            pltpu.VMEM((ts, MIX_WIDTH), F32),
            pltpu.VMEM((SUBLANES, MIX_WIDTH), F32),
        ],
        compiler_params=pltpu.CompilerParams(
            dimension_semantics=("arbitrary", "arbitrary"), vmem_limit_bytes=VMEM_LIMIT_BYTES),
        name="rglru_mix",
    )(h, g, w_in, conv_w, conv_b, wg, b_rg, b_ig, lam, mqg, bd, kbd, vbd, w_out)


def _split3(x):
    p0 = x.astype(BF16)
    r1 = x - p0.astype(F32)
    p1 = r1.astype(BF16)
    p2 = (r1 - p1.astype(F32)).astype(BF16)
    return p0, p1, p2


def _fox_proj_kernel(h_ref, g_ref, w_ref, qg_ref, kg_ref, mqg_ref, bf_ref, bd_ref, tri_ref,
                     q_ref, k_ref, vt_ref, qm_ref, e_ref, base_ref, rel_ref, abs_ref, *, tiles_per_attn):
    t = pl.program_id(1)

    @pl.when(t == 0)
    def _():
        abs_ref[...] = jnp.zeros((1, F_PAD), F32)

    @pl.when(t % tiles_per_attn == 0)
    def _():
        rel_ref[...] = jnp.zeros((1, F_PAD), F32)

    hn = _rms(h_ref[0], g_ref[...]).astype(BF16)
    proj = _dot(hn, w_ref[...])
    w = MIX_WIDTH
    q_ref[0] = (_head_rms(proj[:, :w], qg_ref[...], bd_ref) * SCORE_SCALE).astype(BF16)
    k_ref[0] = _head_rms(proj[:, w:2 * w], kg_ref[...], bd_ref).astype(BF16)
    vt_ref[0] = proj[:, 2 * w:3 * w].T.astype(BF16)
    qm = proj[:, 3 * w:3 * w + MEM_WIDTH]
    qm_ref[0] = (_head_rms(qm, mqg_ref[...], bd_ref) * SCORE_SCALE).astype(BF16)

    log_f = -_softplus(-(proj[:, 3 * w + MEM_WIDTH:] + bf_ref[...]))
    tri = tri_ref[...]
    rel = rel_ref[...] + sum(_dot(tri, p) for p in _split3(log_f))
    last = rel[rel.shape[0] - 1:, :]
    lane = lax.broadcasted_iota(jnp.int32, rel.shape, 1)
    e0, e1, e2 = _split3(jnp.where(lane < MIX_HEADS, -rel, 0.0))
    e_ref[0] = (e0.astype(F32) + pltpu.roll(e1.astype(F32), E_STRIDE, 1)
                + pltpu.roll(e2.astype(F32), 2 * E_STRIDE, 1)).astype(BF16)
    base_ref[0, 0] = abs_ref[...]
    rel_ref[...] = last

    @pl.when(t % tiles_per_attn == tiles_per_attn - 1)
    def _():
        abs_ref[...] = abs_ref[...] + last


def _fox_proj(h, g, w, qg, kg, mqg, bf, bd, tri, ts, tq):
    nb, s, _ = h.shape
    const = lambda shape: pl.BlockSpec(shape, lambda b, i: (0,) * len(shape), pipeline_mode=pl.Buffered(1))
    tile = lambda width: pl.BlockSpec((1, ts, width), lambda b, i: (b, i, 0))
    return pl.pallas_call(
        functools.partial(_fox_proj_kernel, tiles_per_attn=tq // ts),
        grid=(nb, s // ts),
        in_specs=[
            tile(D_MODEL),
            const((1, D_MODEL)),
            const((D_MODEL, FOX_COLS)),
            const((1, MIX_WIDTH)),
            const((1, MIX_WIDTH)),
            const((1, MEM_WIDTH)),
            const((1, F_PAD)),
            const((MXU_DIM, MXU_DIM)),
            const((ts, ts)),
        ],
        out_specs=[
            tile(MIX_WIDTH), tile(MIX_WIDTH),
            pl.BlockSpec((1, MIX_WIDTH, ts), lambda b, i: (b, 0, i)),
            tile(MEM_WIDTH), tile(F_PAD),
            pl.BlockSpec((1, 1, 1, F_PAD), lambda b, i: (b, i, 0, 0)),
        ],
        out_shape=[
            jax.ShapeDtypeStruct((nb, s, MIX_WIDTH), BF16),
            jax.ShapeDtypeStruct((nb, s, MIX_WIDTH), BF16),
            jax.ShapeDtypeStruct((nb, MIX_WIDTH, s), BF16),
            jax.ShapeDtypeStruct((nb, s, MEM_WIDTH), BF16),
            jax.ShapeDtypeStruct((nb, s, F_PAD), BF16),
            jax.ShapeDtypeStruct((nb, s // ts, 1, F_PAD), F32),
        ],
        scratch_shapes=[pltpu.VMEM((1, F_PAD), F32), pltpu.VMEM((1, F_PAD), F32)],
        compiler_params=pltpu.CompilerParams(
            dimension_semantics=("arbitrary", "arbitrary"), vmem_limit_bytes=VMEM_LIMIT_BYTES),
        name="fox_proj",
    )(h, g, w, qg, kg, mqg, bf, bd, tri)


def _fox_attn_kernel(base_ref, q_ref, k_ref, e_ref, vt_ref, o_ref, qc_ref, m_ref, l_ref, acc_ref,
                     *, tq, n_tiles):
    b = pl.program_id(0)
    hp = pl.program_id(1)
    qi = pl.program_id(2)
    lane = lax.broadcasted_iota(jnp.int32, (tq, LANES), 1)
    qf = q_ref[0].astype(F32)
    for a in range(2):
        h = 2 * hp + a
        own = (lane < HEAD_DIM) if a == 0 else (lane >= HEAD_DIM)
        sel = (jnp.bitwise_and(lane, E_STRIDE - 1) == h) & (lane < 3 * E_STRIDE)
        qc_ref[a] = jnp.concatenate(
            [jnp.where(own, qf, 0.0), jnp.where(sel, 1.0, 0.0)], axis=-1).astype(BF16)
    m_ref[...] = jnp.full(m_ref.shape, -jnp.inf, F32)
    l_ref[...] = jnp.zeros(l_ref.shape, F32)
    acc_ref[...] = jnp.zeros(acc_ref.shape, F32)

    def step(j, masked):
        k_start = pl.multiple_of(j * tq, tq)
        kc = jnp.concatenate([k_ref[0, pl.ds(k_start, tq), :], e_ref[0, pl.ds(k_start, tq), :]], axis=-1)
        vt = vt_ref[0, :, pl.ds(k_start, tq)]
        for a in range(2):
            h = 2 * hp + a
            delta = (base_ref[(b * n_tiles + qi) * E_STRIDE + h]
                     - base_ref[(b * n_tiles + j) * E_STRIDE + h])
            raw = _dot_nt(kc, qc_ref[a])
            if masked:
                key = lax.broadcasted_iota(jnp.int32, (tq, tq), 0)
                qry = lax.broadcasted_iota(jnp.int32, (tq, tq), 1)
                raw = jnp.where(key <= qry, raw, -jnp.inf)
            m_prev = m_ref[a]
            m_new = jnp.maximum(m_prev, jnp.max(raw, axis=0, keepdims=True) + delta)
            alpha = jnp.exp(m_prev - m_new)
            p = jnp.exp(raw - (m_new - delta))
            l_ref[a] = alpha * l_ref[a] + jnp.sum(p, axis=0, keepdims=True)
            acc_ref[a] = alpha * acc_ref[a] + _dot(vt[a * HEAD_DIM:(a + 1) * HEAD_DIM, :], p.astype(BF16))
            m_ref[a] = m_new

    def body(j, carry):
        step(j, False)
        return carry

    lax.fori_loop(0, qi, body, 0)
    step(qi, True)
    out_t = jnp.concatenate([acc_ref[a] * (1.0 / l_ref[a]) for a in range(2)], axis=0)
    o_ref[0] = out_t.T.astype(BF16)


def _fox_attention(base, q, k, e, vt, tq):
    nb, s, _ = q.shape
    return pl.pallas_call(
        functools.partial(_fox_attn_kernel, tq=tq, n_tiles=s // tq),
        grid=(nb, MIX_WIDTH // LANES, s // tq),
        in_specs=[
            pl.BlockSpec(memory_space=pltpu.SMEM),
            pl.BlockSpec((1, tq, LANES), lambda b, p, i: (b, i, p)),
            pl.BlockSpec((1, s, LANES), lambda b, p, i: (b, 0, p)),
            pl.BlockSpec((1, s, F_PAD), lambda b, p, i: (b, 0, 0)),
            pl.BlockSpec((1, LANES, s), lambda b, p, i: (b, p, 0)),
        ],
        out_specs=pl.BlockSpec((1, tq, LANES), lambda b, p, i: (b, i, p)),
        out_shape=jax.ShapeDtypeStruct((nb, s, MIX_WIDTH), BF16),
        scratch_shapes=[
            pltpu.VMEM((2, tq, 2 * LANES), BF16),
            pltpu.VMEM((2, 1, tq), F32),
            pltpu.VMEM((2, 1, tq), F32),
            pltpu.VMEM((2, HEAD_DIM, tq), F32),
        ],
        compiler_params=pltpu.CompilerParams(
            dimension_semantics=("arbitrary", "arbitrary", "arbitrary"),
            vmem_limit_bytes=VMEM_LIMIT_BYTES),
        name="fox_attention",
    )(base, q, k, e, vt)


def _fox_out_kernel(h_ref, tok_ref, qm_ref, kbd_ref, vbd_ref, wout_ref, o_ref):
    cross = _cross_attention(qm_ref[0], kbd_ref, vbd_ref)
    o_ref[0] = _mix_out(h_ref[0], tok_ref[0], cross, wout_ref)


def _fox_out(h, tok, qm, kbd, vbd, w_out, ts):
    nb, s, _ = h.shape
    tile = lambda width: pl.BlockSpec((1, ts, width), lambda b, i: (b, i, 0))
    per_batch = lambda shape: pl.BlockSpec((1,) + shape, lambda b, i: (b, 0, 0))
    return pl.pallas_call(
        _fox_out_kernel,
        grid=(nb, s // ts),
        in_specs=[
            tile(D_MODEL), tile(MIX_WIDTH), tile(MEM_WIDTH),
            per_batch((MEM_WIDTH, MEM_HEADS * N_MEM)),
            per_batch((MEM_HEADS * N_MEM, MEM_WIDTH)),
            pl.BlockSpec((D_MODEL, D_MODEL), lambda b, i: (0, 0), pipeline_mode=pl.Buffered(1)),
        ],
        out_specs=tile(D_MODEL),
        out_shape=jax.ShapeDtypeStruct(h.shape, F32),
        compiler_params=pltpu.CompilerParams(
            dimension_semantics=("arbitrary", "arbitrary"), vmem_limit_bytes=VMEM_LIMIT_BYTES),
        name="fox_out",
    )(h, tok, qm, kbd, vbd, w_out)


def _row(v):
    return v.reshape(1, -1).astype(F32)


def _tile_heads(g, heads):
    return jnp.tile(g.astype(F32), heads).reshape(1, -1)


def _gate_weights(w_rg, w_ig):
    def block_diag(w):
        groups = w.reshape(MIX_HEADS // HEADS_PER_MXU_TILE, HEADS_PER_MXU_TILE, HEAD_DIM, HEAD_DIM)
        eye = jnp.eye(HEADS_PER_MXU_TILE, dtype=w.dtype)
        bd = jnp.einsum('ghij,hk->ghikj', groups, eye)
        return bd.reshape(-1, MXU_DIM, MXU_DIM)
    return jnp.concatenate([block_diag(w_rg), block_diag(w_ig)], axis=-1).astype(BF16)


def _fox_weights(w):
    qkv = w[:, :3 * MIX_WIDTH]
    f = w[:, 3 * MIX_WIDTH:3 * MIX_WIDTH + MIX_HEADS]
    qm = w[:, 3 * MIX_WIDTH + MIX_HEADS:]
    f = jnp.pad(f, ((0, 0), (0, F_PAD - MIX_HEADS)))
    return jnp.concatenate([qkv, qm, f], axis=-1).astype(BF16)


def kernel(x, mem, mem_norm_g, mem_w_kv, mem_k_norm_g, ffn1_norm_g, ffn1_w_in, ffn1_w_out, mix_norm_g, mix_w_out, memq_norm_g, ffn2_norm_g, ffn2_w_in, ffn2_w_out, lru_w_in, lru_conv_w, lru_conv_b, lru_w_rg, lru_b_rg, lru_w_ig, lru_b_ig, lru_lambda, fox_w_in, fox_b_f, fox_q_norm_g, fox_k_norm_g):
    nb, s, d = x.shape
    assert d == D_MODEL and mem.shape[1:] == (N_MEM, D_MODEL)
    ts = min(ROW_TILE, s)
    tq = min(ATTN_TILE, s)
    assert s % ts == 0 and s % tq == 0 and (nb * s) % ts == 0

    head_id = jnp.arange(MXU_DIM) // HEAD_DIM
    bd = (head_id[:, None] == head_id[None, :]).astype(BF16)
    tri = (jnp.arange(ts)[:, None] >= jnp.arange(ts)[None, :]).astype(BF16)

    kbd, vbd = _memory_kv(mem, _row(mem_norm_g), mem_w_kv.astype(BF16),
                          _tile_heads(mem_k_norm_g, MEM_HEADS), bd)

    def ffn(h, g, w_in, w_out):
        out = _ffn(h.reshape(nb * s, d), _row(g), w_in.astype(BF16), w_out.astype(BF16), ts)
        return out.reshape(nb, s, d)

    h = x
    h = ffn(h, ffn1_norm_g[0], ffn1_w_in[0], ffn1_w_out[0])
    h = _lru_mix(h, _row(mix_norm_g[0]), lru_w_in[0].astype(BF16), lru_conv_w[0].astype(F32),
                 _row(lru_conv_b[0]), _gate_weights(lru_w_rg[0], lru_w_ig[0]), _row(lru_b_rg[0]),
                 _row(lru_b_ig[0]), _row(lru_lambda[0]), _tile_heads(memq_norm_g[0], MEM_HEADS),
                 bd, kbd, vbd, mix_w_out[0].astype(BF16), ts)
    h = ffn(h, ffn2_norm_g[0], ffn2_w_in[0], ffn2_w_out[0])
    h = ffn(h, ffn1_norm_g[1], ffn1_w_in[1], ffn1_w_out[1])
    bf = jnp.pad(fox_b_f[0].astype(F32), (0, F_PAD - MIX_HEADS)).reshape(1, F_PAD)
    q, k, vt, qm, e, base = _fox_proj(h, _row(mix_norm_g[1]), _fox_weights(fox_w_in[0]),
                                      _tile_heads(fox_q_norm_g[0], MIX_HEADS),
                                      _tile_heads(fox_k_norm_g[0], MIX_HEADS),
                                      _tile_heads(memq_norm_g[1], MEM_HEADS), bf, bd, tri, ts, tq)
    base = base[:, ::tq // ts, 0, :E_STRIDE].reshape(-1)
    tok = _fox_attention(base, q, k, e, vt, tq)
    h = _fox_out(h, tok, qm, kbd, vbd, mix_w_out[1].astype(BF16), ts)
    h = ffn(h, ffn2_norm_g[1], ffn2_w_in[1], ffn2_w_out[1])
    return h
```

```python
import functools

import jax
import jax.numpy as jnp
from jax import lax
from jax.experimental import pallas as pl
from jax.experimental.pallas import tpu as pltpu

F32 = jnp.float32
BF16 = jnp.bfloat16

D_MODEL = 1024
HEAD_DIM = 64
MIX_HEADS = 12
MIX_WIDTH = MIX_HEADS * HEAD_DIM
MEM_HEADS = 4
MEM_WIDTH = MEM_HEADS * HEAD_DIM
N_MEM = 256
D_FF = 2816
CONV_WIDTH = 4
LRU_C = 8.0
NORM_EPS = 1e-6
SCORE_SCALE = HEAD_DIM ** -0.5
LOG2E = 1.4426950408889634
VT_ROWS = 80

LANES = 128
SUBLANES = 8
MXU_DIM = 256
HEADS_PER_MXU_TILE = MXU_DIM // HEAD_DIM
F_PAD = LANES
FOX_COLS = 3 * MIX_WIDTH + MEM_WIDTH + F_PAD
E_STRIDE = 16
VMEM_LIMIT_BYTES = 56 * 1024 * 1024

ROW_TILE = 256
ATTN_TILE = 512


def _dot(a, b):
    return jnp.dot(a, b, preferred_element_type=F32)


def _dot_nt(a, b):
    return lax.dot_general(a, b, (((1,), (1,)), ((), ())), preferred_element_type=F32)


def _rms(x, g):
    ms = jnp.mean(x * x, axis=-1, keepdims=True)
    return x * lax.rsqrt(ms + NORM_EPS) * g


def _head_rms(x, g, bd_ref):
    outs = []
    for c in range(x.shape[1] // MXU_DIM):
        xc = x[:, c * MXU_DIM:(c + 1) * MXU_DIM]
        x2 = xc * xc
        hi = x2.astype(BF16)
        lo = (x2 - hi.astype(F32)).astype(BF16)
        ms = (_dot(hi, bd_ref[...]) + _dot(lo, bd_ref[...])) * (1.0 / HEAD_DIM)
        outs.append(xc * lax.rsqrt(ms + NORM_EPS))
    y = outs[0] if len(outs) == 1 else jnp.concatenate(outs, axis=-1)
    return y * g


def _softplus(x):
    return jnp.maximum(x, 0.0) + jnp.log1p(jnp.exp(-jnp.abs(x)))


def _memkv_kernel(mem_ref, g_ref, w_ref, kg_ref, bd_ref, kbd_ref, vbd_ref):
    xn = _rms(mem_ref[0], g_ref[...]).astype(BF16)
    kv = _dot(xn, w_ref[...])
    k = _head_rms(kv[:, :MEM_WIDTH], kg_ref[...], bd_ref)
    v = kv[:, MEM_WIDTH:]
    kt = k.T
    shape = (MEM_WIDTH, N_MEM)
    row_head = lax.shift_right_logical(lax.broadcasted_iota(jnp.int32, shape, 0), 6)
    col_head = lax.shift_right_logical(lax.broadcasted_iota(jnp.int32, (N_MEM, MEM_WIDTH), 1), 6)
    for h in range(MEM_HEADS):
        kbd_ref[0, :, h * N_MEM:(h + 1) * N_MEM] = jnp.where(row_head == h, kt, 0.0).astype(BF16)
        vbd_ref[0, h * N_MEM:(h + 1) * N_MEM, :] = jnp.where(col_head == h, v, 0.0).astype(BF16)


def _memory_kv(mem, g, w_kv, kg, bd):
    nb = mem.shape[0]
    const = lambda shape: pl.BlockSpec(shape, lambda b: (0,) * len(shape))
    return pl.pallas_call(
        _memkv_kernel,
        grid=(nb,),
        in_specs=[
            pl.BlockSpec((1, N_MEM, D_MODEL), lambda b: (b, 0, 0)),
            const((1, D_MODEL)),
            const((D_MODEL, 2 * MEM_WIDTH)),
            const((1, MEM_WIDTH)),
            const((MXU_DIM, MXU_DIM)),
        ],
        out_specs=[
            pl.BlockSpec((1, MEM_WIDTH, MEM_HEADS * N_MEM), lambda b: (b, 0, 0)),
            pl.BlockSpec((1, MEM_HEADS * N_MEM, MEM_WIDTH), lambda b: (b, 0, 0)),
        ],
        out_shape=[
            jax.ShapeDtypeStruct((nb, MEM_WIDTH, MEM_HEADS * N_MEM), BF16),
            jax.ShapeDtypeStruct((nb, MEM_HEADS * N_MEM, MEM_WIDTH), BF16),
        ],
        compiler_params=pltpu.CompilerParams(
            dimension_semantics=("arbitrary",), vmem_limit_bytes=VMEM_LIMIT_BYTES),
        name="memory_kv",
    )(mem, g, w_kv, kg, bd)


def _ffn_kernel(h_ref, g_ref, win_ref, wout_ref, o_ref):
    x = h_ref[...]
    xn = _rms(x, g_ref[...]).astype(BF16)
    gu = _dot(xn, win_ref[...])
    gate = gu[:, :D_FF]
    up = gu[:, D_FF:]
    act = (jax.nn.silu(gate) * up).astype(BF16)
    o_ref[...] = x + 0.5 * _dot(act, wout_ref[...])


def _ffn(h, g, w_in, w_out, tm):
    t = h.shape[0]
    resident = lambda shape: pl.BlockSpec(shape, lambda i: (0, 0), pipeline_mode=pl.Buffered(1))
    return pl.pallas_call(
        _ffn_kernel,
        grid=(t // tm,),
        in_specs=[
            pl.BlockSpec((tm, D_MODEL), lambda i: (i, 0)),
            resident((1, D_MODEL)),
            resident((D_MODEL, 2 * D_FF)),
            resident((D_FF, D_MODEL)),
        ],
        out_specs=pl.BlockSpec((tm, D_MODEL), lambda i: (i, 0)),
        out_shape=jax.ShapeDtypeStruct((t, D_MODEL), F32),
        compiler_params=pltpu.CompilerParams(
            dimension_semantics=("arbitrary",), vmem_limit_bytes=VMEM_LIMIT_BYTES),
        name="swiglu_ffn",
    )(h, g, w_in, w_out)


def _cross_attention(qm, kbd_ref, vbd_ref):
    s = _dot(qm, kbd_ref[0])
    probs = []
    for h in range(MEM_HEADS):
        sh = s[:, h * N_MEM:(h + 1) * N_MEM]
        e = jnp.exp(sh - jnp.max(sh, axis=-1, keepdims=True))
        inv = 1.0 / jnp.sum(e, axis=-1, keepdims=True)
        probs.append((e * inv).astype(BF16))
    return _dot(jnp.concatenate(probs, axis=-1), vbd_ref[0])


def _mix_out(h, tok, cross, wout_ref):
    cat = jnp.concatenate([tok.astype(BF16), cross.astype(BF16)], axis=-1)
    return h + _dot(cat, wout_ref[...])


def _lru_kernel(h_ref, g_ref, win_ref, cw_ref, cb_ref, wg_ref, brg_ref, big_ref, lam_ref,
                mqg_ref, bd_ref, kbd_ref, vbd_ref, wout_ref, o_ref,
                xs_ref, a_ref, b_ref, hprev_ref, *, ts):
    @pl.when(pl.program_id(1) == 0)
    def _():
        xs_ref[0:SUBLANES, :] = jnp.zeros((SUBLANES, MIX_WIDTH), F32)
        hprev_ref[...] = jnp.zeros((SUBLANES, MIX_WIDTH), F32)

    h = h_ref[0]
    hn = _rms(h, g_ref[...]).astype(BF16)
    proj = _dot(hn, win_ref[...])
    g_br = proj[:, MIX_WIDTH:2 * MIX_WIDTH]
    q_mem = proj[:, 2 * MIX_WIDTH:]

    xs_ref[SUBLANES:SUBLANES + ts, :] = proj[:, :MIX_WIDTH]
    xc = cb_ref[...]
    for tap in range(CONV_WIDTH):
        off = SUBLANES - (CONV_WIDTH - 1) + tap
        xc = xc + xs_ref[off:off + ts, :] * cw_ref[tap:tap + 1, :]
    xs_ref[0:SUBLANES, :] = xs_ref[ts:ts + SUBLANES, :]

    xcb = xc.astype(BF16)
    r_parts, i_parts = [], []
    for c in range(MIX_WIDTH // MXU_DIM):
        gg = _dot(xcb[:, c * MXU_DIM:(c + 1) * MXU_DIM], wg_ref[c])
        r_parts.append(gg[:, :MXU_DIM])
        i_parts.append(gg[:, MXU_DIM:])
    r = jax.nn.sigmoid(jnp.concatenate(r_parts, axis=-1) + brg_ref[...])
    gi = jax.nn.sigmoid(jnp.concatenate(i_parts, axis=-1) + big_ref[...])
    log_a = -LRU_C * r * _softplus(-lam_ref[...])
    t = jnp.tanh(-log_a)
    a_ref[...] = jnp.exp(log_a)
    b_ref[...] = jnp.sqrt(2.0 * t / (1.0 + t)) * (gi * xc)

    row = lax.broadcasted_iota(jnp.int32, (SUBLANES, MIX_WIDTH), 0)

    def scan_block(i, hprev):
        start = pl.multiple_of(i * SUBLANES, SUBLANES)
        a = a_ref[pl.ds(start, SUBLANES), :]
        b = b_ref[pl.ds(start, SUBLANES), :]
        for k in (1, 2, 4):
            keep = row >= k
            a_sh = jnp.where(keep, pltpu.roll(a, k, 0), 1.0)
            b_sh = jnp.where(keep, pltpu.roll(b, k, 0), 0.0)
            b = b + a * b_sh
            a = a * a_sh
        hs = b + a * hprev
        b_ref[pl.ds(start, SUBLANES), :] = hs
        return jnp.broadcast_to(hs[SUBLANES - 1:SUBLANES, :], (SUBLANES, MIX_WIDTH))

    hprev_ref[...] = lax.fori_loop(0, ts // SUBLANES, scan_block, hprev_ref[...], unroll=4)

    tok = b_ref[...] * jax.nn.gelu(g_br)
    qm = (_head_rms(q_mem, mqg_ref[...], bd_ref) * SCORE_SCALE).astype(BF16)
    cross = _cross_attention(qm, kbd_ref, vbd_ref)
    o_ref[0] = _mix_out(h, tok, cross, wout_ref)


def _lru_mix(h, g, w_in, conv_w, conv_b, wg, b_rg, b_ig, lam, mqg, bd, kbd, vbd, w_out, ts):
    nb, s, _ = h.shape
    const = lambda shape: pl.BlockSpec(shape, lambda b, i: (0,) * len(shape), pipeline_mode=pl.Buffered(1))
    per_batch = lambda shape: pl.BlockSpec((1,) + shape, lambda b, i: (b, 0, 0))
    return pl.pallas_call(
        functools.partial(_lru_kernel, ts=ts),
        grid=(nb, s // ts),
        in_specs=[
            pl.BlockSpec((1, ts, D_MODEL), lambda b, i: (b, i, 0)),
            const((1, D_MODEL)),
            const((D_MODEL, 2 * MIX_WIDTH + MEM_WIDTH)),
            const((CONV_WIDTH, MIX_WIDTH)),
            const((1, MIX_WIDTH)),
            const((MIX_WIDTH // MXU_DIM, MXU_DIM, 2 * MXU_DIM)),
            const((1, MIX_WIDTH)),
            const((1, MIX_WIDTH)),
            const((1, MIX_WIDTH)),
            const((1, MEM_WIDTH)),
            const((MXU_DIM, MXU_DIM)),
            per_batch((MEM_WIDTH, MEM_HEADS * N_MEM)),
            per_batch((MEM_HEADS * N_MEM, MEM_WIDTH)),
            const((D_MODEL, D_MODEL)),
        ],
        out_specs=pl.BlockSpec((1, ts, D_MODEL), lambda b, i: (b, i, 0)),
        out_shape=jax.ShapeDtypeStruct(h.shape, F32),
        scratch_shapes=[
            pltpu.VMEM((ts + SUBLANES, MIX_WIDTH), F32),
            pltpu.VMEM((ts, MIX_WIDTH), F32),
            pltpu.VMEM((ts, MIX_WIDTH), F32),
            pltpu.VMEM((SUBLANES, MIX_WIDTH), F32),
        ],
        compiler_params=pltpu.CompilerParams(
            dimension_semantics=("arbitrary", "arbitrary"), vmem_limit_bytes=VMEM_LIMIT_BYTES),
        name="rglru_mix",
    )(h, g, w_in, conv_w, conv_b, wg, b_rg, b_ig, lam, mqg, bd, kbd, vbd, w_out)


def _split3(x):
    p0 = x.astype(BF16)
    r1 = x - p0.astype(F32)
    p1 = r1.astype(BF16)
    p2 = (r1 - p1.astype(F32)).astype(BF16)
    return p0, p1, p2


def _fox_proj_kernel(h_ref, g_ref, w_ref, qg_ref, kg_ref, mqg_ref, bf_ref, bd_ref, tri_ref,
                     q_ref, k_ref, vt_ref, qm_ref, e_ref, base_ref, rel_ref, abs_ref, *, tiles_per_attn):
    t = pl.program_id(1)

    @pl.when(t == 0)
    def _():
        abs_ref[...] = jnp.zeros((1, F_PAD), F32)

    @pl.when(t % tiles_per_attn == 0)
    def _():
        rel_ref[...] = jnp.zeros((1, F_PAD), F32)

    hn = _rms(h_ref[0], g_ref[...]).astype(BF16)
    proj = _dot(hn, w_ref[...])
    w = MIX_WIDTH
    q_ref[0] = (_head_rms(proj[:, :w], qg_ref[...], bd_ref) * (SCORE_SCALE * LOG2E)).astype(BF16)
    k_ref[0] = _head_rms(proj[:, w:2 * w], kg_ref[...], bd_ref).astype(BF16)
    vt = proj[:, 2 * w:3 * w].T.astype(BF16)
    pad_rows = VT_ROWS - HEAD_DIM
    ones_row = jnp.where(lax.broadcasted_iota(jnp.int32, (pad_rows, vt.shape[1]), 0) == 0, 1.0, 0.0)
    for hd in range(MIX_HEADS):
        vt_ref[0, hd, :HEAD_DIM, :] = vt[hd * HEAD_DIM:(hd + 1) * HEAD_DIM, :]
        vt_ref[0, hd, HEAD_DIM:, :] = ones_row.astype(BF16)
    qm = proj[:, 3 * w:3 * w + MEM_WIDTH]
    qm_ref[0] = (_head_rms(qm, mqg_ref[...], bd_ref) * SCORE_SCALE).astype(BF16)

    log_f = -_softplus(-(proj[:, 3 * w + MEM_WIDTH:] + bf_ref[...]))
    tri = tri_ref[...]
    rel = rel_ref[...] + sum(_dot(tri, p) for p in _split3(log_f))
    last = rel[rel.shape[0] - 1:, :]
    lane = lax.broadcasted_iota(jnp.int32, rel.shape, 1)
    e0, e1, e2 = _split3(jnp.where(lane < MIX_HEADS, -LOG2E * rel, 0.0))
    e_ref[0] = (e0.astype(F32) + pltpu.roll(e1.astype(F32), E_STRIDE, 1)
                + pltpu.roll(e2.astype(F32), 2 * E_STRIDE, 1)).astype(BF16)
    base_ref[0, 0] = LOG2E * abs_ref[...]
    rel_ref[...] = last

    @pl.when(t % tiles_per_attn == tiles_per_attn - 1)
    def _():
        abs_ref[...] = abs_ref[...] + last


def _fox_proj(h, g, w, qg, kg, mqg, bf, bd, tri, ts, tq):
    nb, s, _ = h.shape
    const = lambda shape: pl.BlockSpec(shape, lambda b, i: (0,) * len(shape), pipeline_mode=pl.Buffered(1))
    tile = lambda width: pl.BlockSpec((1, ts, width), lambda b, i: (b, i, 0))
    return pl.pallas_call(
        functools.partial(_fox_proj_kernel, tiles_per_attn=tq // ts),
        grid=(nb, s // ts),
        in_specs=[
            tile(D_MODEL),
            const((1, D_MODEL)),
            const((D_MODEL, FOX_COLS)),
            const((1, MIX_WIDTH)),
            const((1, MIX_WIDTH)),
            const((1, MEM_WIDTH)),
            const((1, F_PAD)),
            const((MXU_DIM, MXU_DIM)),
            const((ts, ts)),
        ],
        out_specs=[
            tile(MIX_WIDTH), tile(MIX_WIDTH),
            pl.BlockSpec((1, MIX_HEADS, VT_ROWS, ts), lambda b, i: (b, 0, 0, i)),
            tile(MEM_WIDTH), tile(F_PAD),
            pl.BlockSpec((1, 1, 1, F_PAD), lambda b, i: (b, i, 0, 0)),
        ],
        out_shape=[
            jax.ShapeDtypeStruct((nb, s, MIX_WIDTH), BF16),
            jax.ShapeDtypeStruct((nb, s, MIX_WIDTH), BF16),
            jax.ShapeDtypeStruct((nb, MIX_HEADS, VT_ROWS, s), BF16),
            jax.ShapeDtypeStruct((nb, s, MEM_WIDTH), BF16),
            jax.ShapeDtypeStruct((nb, s, F_PAD), BF16),
            jax.ShapeDtypeStruct((nb, s // ts, 1, F_PAD), F32),
        ],
        scratch_shapes=[pltpu.VMEM((1, F_PAD), F32), pltpu.VMEM((1, F_PAD), F32)],
        compiler_params=pltpu.CompilerParams(
            dimension_semantics=("arbitrary", "arbitrary"), vmem_limit_bytes=VMEM_LIMIT_BYTES),
        name="fox_proj",
    )(h, g, w, qg, kg, mqg, bf, bd, tri)


def _fox_attn_kernel(base_ref, q_ref, k_ref, e_ref, vt_ref, o_ref, qc_ref, m_ref, acc_ref,
                     s_ref, mc_ref, p_ref, al_ref, *, tq, n_tiles):
    b = pl.program_id(0)
    hp = pl.program_id(1)
    qi = pl.program_id(2)
    lane = lax.broadcasted_iota(jnp.int32, (tq, LANES), 1)
    qf = q_ref[0].astype(F32)
    for a in range(2):
        h = 2 * hp + a
        own = (lane < HEAD_DIM) if a == 0 else (lane >= HEAD_DIM)
        sel = (jnp.bitwise_and(lane, E_STRIDE - 1) == h) & (lane < 3 * E_STRIDE)
        qc_ref[a] = jnp.concatenate(
            [jnp.where(own, qf, 0.0), jnp.where(sel, 1.0, 0.0)], axis=-1).astype(BF16)
    m_ref[...] = jnp.full(m_ref.shape, -jnp.inf, F32)
    acc_ref[...] = jnp.zeros(acc_ref.shape, F32)

    def scores(a, j, slot, masked):
        k_start = pl.multiple_of(j * tq, tq)
        kc = jnp.concatenate([k_ref[0, pl.ds(k_start, tq), :], e_ref[0, pl.ds(k_start, tq), :]], axis=-1)
        raw = _dot_nt(kc, qc_ref[a])
        if masked:
            key = lax.broadcasted_iota(jnp.int32, (tq, tq), 0)
            qry = lax.broadcasted_iota(jnp.int32, (tq, tq), 1)
            raw = jnp.where(key <= qry, raw, -jnp.inf)
        s_ref[a, slot] = raw
        mc_ref[a, slot] = jnp.max(raw, axis=0, keepdims=True)

    def softmax(a, j, slot):
        h = 2 * hp + a
        delta = (base_ref[(b * n_tiles + qi) * E_STRIDE + h]
                 - base_ref[(b * n_tiles + j) * E_STRIDE + h])
        m_prev = m_ref[a]
        m_new = jnp.maximum(m_prev, mc_ref[a, slot] + delta)
        al_ref[a] = jnp.exp2(m_prev - m_new)
        p_ref[a] = jnp.exp2(s_ref[a, slot] - (m_new - delta)).astype(BF16)
        m_ref[a] = m_new

    def values(a, j):
        k_start = pl.multiple_of(j * tq, tq)
        acc_ref[a] = al_ref[a] * acc_ref[a] + _dot(vt_ref[0, a, :, pl.ds(k_start, tq)], p_ref[a])

    scores(0, qi, 0, True)
    scores(1, qi, 0, True)
    scores(0, 0, 1, False)
    scores(1, 0, 1, False)
    softmax(0, qi, 0)
    softmax(1, qi, 0)

    def round_(t, slot):
        prev = jnp.where(t == 1, qi, t - 2)
        values(0, prev)
        values(1, prev)
        scores(0, t, 1 - slot, False)
        scores(1, t, 1 - slot, False)
        softmax(0, t - 1, slot)
        softmax(1, t - 1, slot)

    def round_pair(i, carry):
        round_(2 * i + 1, 1)
        round_(2 * i + 2, 0)
        return carry

    lax.fori_loop(0, qi // 2, round_pair, 0)

    @pl.when(qi % 2 == 1)
    def _():
        round_(qi, 1)

    last = jnp.where(qi == 0, qi, qi - 1)
    values(0, last)
    values(1, last)
    out_t = jnp.concatenate(
        [acc_ref[a, :HEAD_DIM, :] * (1.0 / acc_ref[a, HEAD_DIM:HEAD_DIM + 1, :]) for a in range(2)], axis=0)
    o_ref[0] = out_t.T.astype(BF16)


def _fox_attention(base, q, k, e, vt, tq):
    nb, s, _ = q.shape
    return pl.pallas_call(
        functools.partial(_fox_attn_kernel, tq=tq, n_tiles=s // tq),
        grid=(nb, MIX_WIDTH // LANES, s // tq),
        in_specs=[
            pl.BlockSpec(memory_space=pltpu.SMEM),
            pl.BlockSpec((1, tq, LANES), lambda b, p, i: (b, i, p)),
            pl.BlockSpec((1, s, LANES), lambda b, p, i: (b, 0, p)),
            pl.BlockSpec((1, s, F_PAD), lambda b, p, i: (b, 0, 0)),
            pl.BlockSpec((1, 2, VT_ROWS, s), lambda b, p, i: (b, p, 0, 0)),
        ],
        out_specs=pl.BlockSpec((1, tq, LANES), lambda b, p, i: (b, i, p)),
        out_shape=jax.ShapeDtypeStruct((nb, s, MIX_WIDTH), BF16),
        scratch_shapes=[
            pltpu.VMEM((2, tq, 2 * LANES), BF16),
            pltpu.VMEM((2, 1, tq), F32),
            pltpu.VMEM((2, VT_ROWS, tq), F32),
            pltpu.VMEM((2, 2, tq, tq), F32),
            pltpu.VMEM((2, 2, 1, tq), F32),
            pltpu.VMEM((2, tq, tq), BF16),
            pltpu.VMEM((2, 1, tq), F32),
        ],
        compiler_params=pltpu.CompilerParams(
            dimension_semantics=("arbitrary", "arbitrary", "arbitrary"),
            vmem_limit_bytes=VMEM_LIMIT_BYTES),
        name="fox_attention",
    )(base, q, k, e, vt)


def _fox_out_kernel(h_ref, tok_ref, qm_ref, kbd_ref, vbd_ref, wout_ref, o_ref):
    cross = _cross_attention(qm_ref[0], kbd_ref, vbd_ref)
    o_ref[0] = _mix_out(h_ref[0], tok_ref[0], cross, wout_ref)


def _fox_out(h, tok, qm, kbd, vbd, w_out, ts):
    nb, s, _ = h.shape
    tile = lambda width: pl.BlockSpec((1, ts, width), lambda b, i: (b, i, 0))
    per_batch = lambda shape: pl.BlockSpec((1,) + shape, lambda b, i: (b, 0, 0))
    return pl.pallas_call(
        _fox_out_kernel,
        grid=(nb, s // ts),
        in_specs=[
            tile(D_MODEL), tile(MIX_WIDTH), tile(MEM_WIDTH),
            per_batch((MEM_WIDTH, MEM_HEADS * N_MEM)),
            per_batch((MEM_HEADS * N_MEM, MEM_WIDTH)),
            pl.BlockSpec((D_MODEL, D_MODEL), lambda b, i: (0, 0), pipeline_mode=pl.Buffered(1)),
        ],
        out_specs=tile(D_MODEL),
        out_shape=jax.ShapeDtypeStruct(h.shape, F32),
        compiler_params=pltpu.CompilerParams(
            dimension_semantics=("arbitrary", "arbitrary"), vmem_limit_bytes=VMEM_LIMIT_BYTES),
        name="fox_out",
    )(h, tok, qm, kbd, vbd, w_out)


def _row(v):
    return v.reshape(1, -1).astype(F32)


def _tile_heads(g, heads):
    return jnp.tile(g.astype(F32), heads).reshape(1, -1)


def _gate_weights(w_rg, w_ig):
    def block_diag(w):
        groups = w.reshape(MIX_HEADS // HEADS_PER_MXU_TILE, HEADS_PER_MXU_TILE, HEAD_DIM, HEAD_DIM)
        eye = jnp.eye(HEADS_PER_MXU_TILE, dtype=w.dtype)
        bd = jnp.einsum('ghij,hk->ghikj', groups, eye)
        return bd.reshape(-1, MXU_DIM, MXU_DIM)
    return jnp.concatenate([block_diag(w_rg), block_diag(w_ig)], axis=-1).astype(BF16)


def _fox_weights(w):
    qkv = w[:, :3 * MIX_WIDTH]
    f = w[:, 3 * MIX_WIDTH:3 * MIX_WIDTH + MIX_HEADS]
    qm = w[:, 3 * MIX_WIDTH + MIX_HEADS:]
    f = jnp.pad(f, ((0, 0), (0, F_PAD - MIX_HEADS)))
    return jnp.concatenate([qkv, qm, f], axis=-1).astype(BF16)


def kernel(x, mem, mem_norm_g, mem_w_kv, mem_k_norm_g, ffn1_norm_g, ffn1_w_in, ffn1_w_out, mix_norm_g, mix_w_out, memq_norm_g, ffn2_norm_g, ffn2_w_in, ffn2_w_out, lru_w_in, lru_conv_w, lru_conv_b, lru_w_rg, lru_b_rg, lru_w_ig, lru_b_ig, lru_lambda, fox_w_in, fox_b_f, fox_q_norm_g, fox_k_norm_g):
    nb, s, d = x.shape
    assert d == D_MODEL and mem.shape[1:] == (N_MEM, D_MODEL)
    ts = min(ROW_TILE, s)
    tq = min(ATTN_TILE, s)
    assert s % ts == 0 and s % tq == 0 and (nb * s) % ts == 0

    head_id = jnp.arange(MXU_DIM) // HEAD_DIM
    bd = (head_id[:, None] == head_id[None, :]).astype(BF16)
    tri = (jnp.arange(ts)[:, None] >= jnp.arange(ts)[None, :]).astype(BF16)

    kbd, vbd = _memory_kv(mem, _row(mem_norm_g), mem_w_kv.astype(BF16),
                          _tile_heads(mem_k_norm_g, MEM_HEADS), bd)

    def ffn(h, g, w_in, w_out):
        out = _ffn(h.reshape(nb * s, d), _row(g), w_in.astype(BF16), w_out.astype(BF16), ts)
        return out.reshape(nb, s, d)

    h = x
    h = ffn(h, ffn1_norm_g[0], ffn1_w_in[0], ffn1_w_out[0])
    h = _lru_mix(h, _row(mix_norm_g[0]), lru_w_in[0].astype(BF16), lru_conv_w[0].astype(F32),
                 _row(lru_conv_b[0]), _gate_weights(lru_w_rg[0], lru_w_ig[0]), _row(lru_b_rg[0]),
                 _row(lru_b_ig[0]), _row(lru_lambda[0]), _tile_heads(memq_norm_g[0], MEM_HEADS),
                 bd, kbd, vbd, mix_w_out[0].astype(BF16), ts)
    h = ffn(h, ffn2_norm_g[0], ffn2_w_in[0], ffn2_w_out[0])
    h = ffn(h, ffn1_norm_g[1], ffn1_w_in[1], ffn1_w_out[1])
    bf = jnp.pad(fox_b_f[0].astype(F32), (0, F_PAD - MIX_HEADS)).reshape(1, F_PAD)
    q, k, vt, qm, e, base = _fox_proj(h, _row(mix_norm_g[1]), _fox_weights(fox_w_in[0]),
                                      _tile_heads(fox_q_norm_g[0], MIX_HEADS),
                                      _tile_heads(fox_k_norm_g[0], MIX_HEADS),
                                      _tile_heads(memq_norm_g[1], MEM_HEADS), bf, bd, tri, ts, tq)
    base = base[:, ::tq // ts, 0, :E_STRIDE].reshape(-1)
    tok = _fox_attention(base, q, k, e, vt, tq)
    h = _fox_out(h, tok, qm, kbd, vbd, mix_w_out[1].astype(BF16), ts)
    h = ffn(h, ffn2_norm_g[1], ffn2_w_in[1], ffn2_w_out[1])
    return h
```

```python
import functools

import jax
import jax.numpy as jnp
from jax import lax
from jax.experimental import pallas as pl
from jax.experimental.pallas import tpu as pltpu

F32 = jnp.float32
BF16 = jnp.bfloat16

D_MODEL = 1024
HEAD_DIM = 64
MIX_HEADS = 12
MIX_WIDTH = MIX_HEADS * HEAD_DIM
MEM_HEADS = 4
MEM_WIDTH = MEM_HEADS * HEAD_DIM
N_MEM = 256
D_FF = 2816
CONV_WIDTH = 4
LRU_C = 8.0
NORM_EPS = 1e-6
SCORE_SCALE = HEAD_DIM ** -0.5
LOG2E = 1.4426950408889634
VT_ROWS = 80

LANES = 128
SUBLANES = 8
MXU_DIM = 256
HEADS_PER_MXU_TILE = MXU_DIM // HEAD_DIM
F_PAD = LANES
FOX_COLS = 3 * MIX_WIDTH + MEM_WIDTH + F_PAD
E_STRIDE = 16
VMEM_LIMIT_BYTES = 56 * 1024 * 1024

ROW_TILE = 256
FFN_ROW_TILE = 512
LRU_ROW_TILE = 512
ATTN_TILE = 512


def _dot(a, b):
    return jnp.dot(a, b, preferred_element_type=F32)


def _dot_nt(a, b):
    return lax.dot_general(a, b, (((1,), (1,)), ((), ())), preferred_element_type=F32)


def _rms(x, g):
    ms = jnp.mean(x * x, axis=-1, keepdims=True)
    return x * lax.rsqrt(ms + NORM_EPS) * g


def _head_rms(x, g, bd_ref):
    outs = []
    for c in range(x.shape[1] // MXU_DIM):
        xc = x[:, c * MXU_DIM:(c + 1) * MXU_DIM]
        x2 = xc * xc
        hi = x2.astype(BF16)
        lo = (x2 - hi.astype(F32)).astype(BF16)
        ms = (_dot(hi, bd_ref[...]) + _dot(lo, bd_ref[...])) * (1.0 / HEAD_DIM)
        outs.append(xc * lax.rsqrt(ms + NORM_EPS))
    y = outs[0] if len(outs) == 1 else jnp.concatenate(outs, axis=-1)
    return y * g


def _softplus(x):
    return jnp.maximum(x, 0.0) + jnp.log1p(jnp.exp(-jnp.abs(x)))


def _memkv_kernel(mem_ref, g_ref, w_ref, kg_ref, bd_ref, kbd_ref, vbd_ref):
    xn = _rms(mem_ref[0], g_ref[...]).astype(BF16)
    kv = _dot(xn, w_ref[...])
    k = _head_rms(kv[:, :MEM_WIDTH], kg_ref[...], bd_ref)
    v = kv[:, MEM_WIDTH:]
    kt = k.T
    shape = (MEM_WIDTH, N_MEM)
    row_head = lax.shift_right_logical(lax.broadcasted_iota(jnp.int32, shape, 0), 6)
    col_head = lax.shift_right_logical(lax.broadcasted_iota(jnp.int32, (N_MEM, MEM_WIDTH), 1), 6)
    for h in range(MEM_HEADS):
        kbd_ref[0, :, h * N_MEM:(h + 1) * N_MEM] = jnp.where(row_head == h, kt, 0.0).astype(BF16)
        vbd_ref[0, h * N_MEM:(h + 1) * N_MEM, :] = jnp.where(col_head == h, v, 0.0).astype(BF16)


def _memory_kv(mem, g, w_kv, kg, bd):
    nb = mem.shape[0]
    const = lambda shape: pl.BlockSpec(shape, lambda b: (0,) * len(shape))
    return pl.pallas_call(
        _memkv_kernel,
        grid=(nb,),
        in_specs=[
            pl.BlockSpec((1, N_MEM, D_MODEL), lambda b: (b, 0, 0)),
            const((1, D_MODEL)),
            const((D_MODEL, 2 * MEM_WIDTH)),
            const((1, MEM_WIDTH)),
            const((MXU_DIM, MXU_DIM)),
        ],
        out_specs=[
            pl.BlockSpec((1, MEM_WIDTH, MEM_HEADS * N_MEM), lambda b: (b, 0, 0)),
            pl.BlockSpec((1, MEM_HEADS * N_MEM, MEM_WIDTH), lambda b: (b, 0, 0)),
        ],
        out_shape=[
            jax.ShapeDtypeStruct((nb, MEM_WIDTH, MEM_HEADS * N_MEM), BF16),
            jax.ShapeDtypeStruct((nb, MEM_HEADS * N_MEM, MEM_WIDTH), BF16),
        ],
        compiler_params=pltpu.CompilerParams(
            dimension_semantics=("arbitrary",), vmem_limit_bytes=VMEM_LIMIT_BYTES),
        name="memory_kv",
    )(mem, g, w_kv, kg, bd)


def _ffn_kernel(h_ref, g_ref, win_ref, wout_ref, o_ref):
    x = h_ref[...]
    xn = _rms(x, g_ref[...]).astype(BF16)
    gu = _dot(xn, win_ref[...])
    gate = gu[:, :D_FF]
    up = gu[:, D_FF:]
    act = (jax.nn.silu(gate) * up).astype(BF16)
    o_ref[...] = x + 0.5 * _dot(act, wout_ref[...])


def _ffn(h, g, w_in, w_out, tm):
    t = h.shape[0]
    resident = lambda shape: pl.BlockSpec(shape, lambda i: (0, 0), pipeline_mode=pl.Buffered(1))
    return pl.pallas_call(
        _ffn_kernel,
        grid=(t // tm,),
        in_specs=[
            pl.BlockSpec((tm, D_MODEL), lambda i: (i, 0)),
            resident((1, D_MODEL)),
            resident((D_MODEL, 2 * D_FF)),
            resident((D_FF, D_MODEL)),
        ],
        out_specs=pl.BlockSpec((tm, D_MODEL), lambda i: (i, 0)),
        out_shape=jax.ShapeDtypeStruct((t, D_MODEL), F32),
        compiler_params=pltpu.CompilerParams(
            dimension_semantics=("arbitrary",), vmem_limit_bytes=VMEM_LIMIT_BYTES),
        name="swiglu_ffn",
    )(h, g, w_in, w_out)


def _cross_attention(qm, kbd_ref, vbd_ref):
    s = _dot(qm, kbd_ref[0])
    probs = []
    for h in range(MEM_HEADS):
        sh = s[:, h * N_MEM:(h + 1) * N_MEM]
        e = jnp.exp(sh - jnp.max(sh, axis=-1, keepdims=True))
        inv = 1.0 / jnp.sum(e, axis=-1, keepdims=True)
        probs.append((e * inv).astype(BF16))
    return _dot(jnp.concatenate(probs, axis=-1), vbd_ref[0])


def _mix_out(h, tok, cross, wout_ref):
    cat = jnp.concatenate([tok.astype(BF16), cross.astype(BF16)], axis=-1)
    return h + _dot(cat, wout_ref[...])


def _lru_kernel(h_ref, g_ref, win_ref, cw_ref, cb_ref, wg_ref, brg_ref, big_ref, lam_ref,
                mqg_ref, bd_ref, kbd_ref, vbd_ref, wout_ref, o_ref,
                xs_ref, gq_ref, a_ref, b_ref, hprev_ref, *, ts):
    half = ts // 2

    @pl.when(pl.program_id(1) == 0)
    def _():
        xs_ref[0:SUBLANES, :] = jnp.zeros((SUBLANES, MIX_WIDTH), F32)
        hprev_ref[...] = jnp.zeros((SUBLANES, MIX_WIDTH), F32)

    def project(k):
        r0 = k * half
        hn = _rms(h_ref[0, r0:r0 + half, :], g_ref[...]).astype(BF16)
        proj = _dot(hn, win_ref[...])
        xs_ref[SUBLANES + r0:SUBLANES + r0 + half, :] = proj[:, :MIX_WIDTH]
        gq_ref[k] = proj[:, MIX_WIDTH:]

    def gates(k):
        r0 = k * half
        xe = xs_ref[r0:r0 + half + SUBLANES, :]
        acc = xe * cw_ref[0:1, :]
        for tap in range(1, CONV_WIDTH):
            acc = pltpu.roll(acc, 1, 0) + xe * cw_ref[tap:tap + 1, :]
        xc = cb_ref[...] + acc[SUBLANES:, :]
        xcb = xc.astype(BF16)
        r_parts, i_parts = [], []
        for c in range(MIX_WIDTH // MXU_DIM):
            gg = _dot(xcb[:, c * MXU_DIM:(c + 1) * MXU_DIM], wg_ref[c])
            r_parts.append(gg[:, :MXU_DIM])
            i_parts.append(gg[:, MXU_DIM:])
        r = jax.nn.sigmoid(jnp.concatenate(r_parts, axis=-1) + brg_ref[...])
        gi = jax.nn.sigmoid(jnp.concatenate(i_parts, axis=-1) + big_ref[...])
        log_a = -LRU_C * r * _softplus(-lam_ref[...])
        t = jnp.tanh(-log_a)
        a = jnp.exp(log_a)
        t2 = 2.0 * t
        root = jnp.where(t > 0.0, t2 * lax.rsqrt(t2 * (1.0 + t)), 0.0)
        bx = root * (gi * xc)
        for lt in range(MIX_WIDTH // LANES):
            a_ref[lt, r0:r0 + half, :] = a[:, lt * LANES:(lt + 1) * LANES]
            b_ref[lt, r0:r0 + half, :] = bx[:, lt * LANES:(lt + 1) * LANES]

    row = lax.broadcasted_iota(jnp.int32, (SUBLANES, LANES), 0)

    def scan(k, hprev):
        blk = SUBLANES * SUBLANES
        hprev = list(hprev)
        for i in range(half // blk):
            base = k * half + i * blk
            for lt in range(MIX_WIDTH // LANES):
                a = [a_ref[lt, pl.ds(base + j, SUBLANES, stride=SUBLANES), :] for j in range(SUBLANES)]
                b = [b_ref[lt, pl.ds(base + j, SUBLANES, stride=SUBLANES), :] for j in range(SUBLANES)]
                hloc, prod = [b[0]], [a[0]]
                for j in range(1, SUBLANES):
                    hloc.append(a[j] * hloc[-1] + b[j])
                    prod.append(a[j] * prod[-1])
                ea, eb = prod[-1], hloc[-1]
                for sh in (1, 2, 4):
                    keep = row >= sh
                    a_sh = jnp.where(keep, pltpu.roll(ea, sh, 0), 1.0)
                    b_sh = jnp.where(keep, pltpu.roll(eb, sh, 0), 0.0)
                    eb = eb + ea * b_sh
                    ea = ea * a_sh
                ends = eb + ea * hprev[lt]
                incoming = jnp.where(row >= 1, pltpu.roll(ends, 1, 0), hprev[lt])
                for j in range(SUBLANES):
                    b_ref[lt, pl.ds(base + j, SUBLANES, stride=SUBLANES), :] = hloc[j] + prod[j] * incoming
                hprev[lt] = jnp.broadcast_to(ends[SUBLANES - 1:SUBLANES, :], (SUBLANES, LANES))
        return hprev

    def finish(k):
        r0 = k * half
        gq = gq_ref[k]
        hs = jnp.concatenate([b_ref[lt, r0:r0 + half, :] for lt in range(MIX_WIDTH // LANES)], axis=-1)
        tok = hs * jax.nn.gelu(gq[:, :MIX_WIDTH])
        qm = (_head_rms(gq[:, MIX_WIDTH:], mqg_ref[...], bd_ref) * SCORE_SCALE).astype(BF16)
        cross = _cross_attention(qm, kbd_ref, vbd_ref)
        o_ref[0, r0:r0 + half, :] = _mix_out(h_ref[0, r0:r0 + half, :], tok, cross, wout_ref)

    project(0)
    project(1)
    gates(0)
    gates(1)
    xs_ref[0:SUBLANES, :] = xs_ref[ts:ts + SUBLANES, :]
    hmid = scan(0, [hprev_ref[:, lt * LANES:(lt + 1) * LANES] for lt in range(MIX_WIDTH // LANES)])
    finish(0)
    hprev_ref[...] = jnp.concatenate(scan(1, hmid), axis=-1)
    finish(1)


def _lru_mix(h, g, w_in, conv_w, conv_b, wg, b_rg, b_ig, lam, mqg, bd, kbd, vbd, w_out, ts):
    nb, s, _ = h.shape
    const = lambda shape: pl.BlockSpec(shape, lambda b, i: (0,) * len(shape), pipeline_mode=pl.Buffered(1))
    per_batch = lambda shape: pl.BlockSpec((1,) + shape, lambda b, i: (b, 0, 0))
    return pl.pallas_call(
        functools.partial(_lru_kernel, ts=ts),
        grid=(nb, s // ts),
        in_specs=[
            pl.BlockSpec((1, ts, D_MODEL), lambda b, i: (b, i, 0)),
            const((1, D_MODEL)),
            const((D_MODEL, 2 * MIX_WIDTH + MEM_WIDTH)),
            const((CONV_WIDTH, MIX_WIDTH)),
            const((1, MIX_WIDTH)),
            const((MIX_WIDTH // MXU_DIM, MXU_DIM, 2 * MXU_DIM)),
            const((1, MIX_WIDTH)),
            const((1, MIX_WIDTH)),
            const((1, MIX_WIDTH)),
            const((1, MEM_WIDTH)),
            const((MXU_DIM, MXU_DIM)),
            per_batch((MEM_WIDTH, MEM_HEADS * N_MEM)),
            per_batch((MEM_HEADS * N_MEM, MEM_WIDTH)),
            const((D_MODEL, D_MODEL)),
        ],
        out_specs=pl.BlockSpec((1, ts, D_MODEL), lambda b, i: (b, i, 0)),
        out_shape=jax.ShapeDtypeStruct(h.shape, F32),
        scratch_shapes=[
            pltpu.VMEM((ts + SUBLANES, MIX_WIDTH), F32),
            pltpu.VMEM((2, ts // 2, MIX_WIDTH + MEM_WIDTH), F32),
            pltpu.VMEM((MIX_WIDTH // LANES, ts, LANES), F32),
            pltpu.VMEM((MIX_WIDTH // LANES, ts, LANES), F32),
            pltpu.VMEM((SUBLANES, MIX_WIDTH), F32),
        ],
        compiler_params=pltpu.CompilerParams(
            dimension_semantics=("arbitrary", "arbitrary"), vmem_limit_bytes=VMEM_LIMIT_BYTES),
        name="rglru_mix",
    )(h, g, w_in, conv_w, conv_b, wg, b_rg, b_ig, lam, mqg, bd, kbd, vbd, w_out)


def _split3(x):
    p0 = x.astype(BF16)
    r1 = x - p0.astype(F32)
    p1 = r1.astype(BF16)
    p2 = (r1 - p1.astype(F32)).astype(BF16)
    return p0, p1, p2


def _fox_proj_kernel(h_ref, g_ref, w_ref, qg_ref, kg_ref, mqg_ref, bf_ref, bd_ref, tri_ref,
                     q_ref, k_ref, vt_ref, qm_ref, e_ref, base_ref, rel_ref, abs_ref, *, tiles_per_attn):
    t = pl.program_id(1)

    @pl.when(t == 0)
    def _():
        abs_ref[...] = jnp.zeros((1, F_PAD), F32)

    @pl.when(t % tiles_per_attn == 0)
    def _():
        rel_ref[...] = jnp.zeros((1, F_PAD), F32)

    hn = _rms(h_ref[0], g_ref[...]).astype(BF16)
    proj = _dot(hn, w_ref[...])
    w = MIX_WIDTH
    q_ref[0] = (_head_rms(proj[:, :w], qg_ref[...], bd_ref) * (SCORE_SCALE * LOG2E)).astype(BF16)
    k_ref[0] = _head_rms(proj[:, w:2 * w], kg_ref[...], bd_ref).astype(BF16)
    vt = proj[:, 2 * w:3 * w].T.astype(BF16)
    pad_rows = VT_ROWS - HEAD_DIM
    ones_row = jnp.where(lax.broadcasted_iota(jnp.int32, (pad_rows, vt.shape[1]), 0) == 0, 1.0, 0.0)
    for hd in range(MIX_HEADS):
        vt_ref[0, hd, :HEAD_DIM, :] = vt[hd * HEAD_DIM:(hd + 1) * HEAD_DIM, :]
        vt_ref[0, hd, HEAD_DIM:, :] = ones_row.astype(BF16)
    qm = proj[:, 3 * w:3 * w + MEM_WIDTH]
    qm_ref[0] = (_head_rms(qm, mqg_ref[...], bd_ref) * SCORE_SCALE).astype(BF16)

    log_f = -_softplus(-(proj[:, 3 * w + MEM_WIDTH:] + bf_ref[...]))
    tri = tri_ref[...]
    rel = rel_ref[...] + sum(_dot(tri, p) for p in _split3(log_f))
    last = rel[rel.shape[0] - 1:, :]
    lane = lax.broadcasted_iota(jnp.int32, rel.shape, 1)
    e0, e1, e2 = _split3(jnp.where(lane < MIX_HEADS, -LOG2E * rel, 0.0))
    e_ref[0] = (e0.astype(F32) + pltpu.roll(e1.astype(F32), E_STRIDE, 1)
                + pltpu.roll(e2.astype(F32), 2 * E_STRIDE, 1)).astype(BF16)
    base_ref[0, 0] = LOG2E * abs_ref[...]
    rel_ref[...] = last

    @pl.when(t % tiles_per_attn == tiles_per_attn - 1)
    def _():
        abs_ref[...] = abs_ref[...] + last


def _fox_proj(h, g, w, qg, kg, mqg, bf, bd, tri, ts, tq):
    nb, s, _ = h.shape
    const = lambda shape: pl.BlockSpec(shape, lambda b, i: (0,) * len(shape), pipeline_mode=pl.Buffered(1))
    tile = lambda width: pl.BlockSpec((1, ts, width), lambda b, i: (b, i, 0))
    return pl.pallas_call(
        functools.partial(_fox_proj_kernel, tiles_per_attn=tq // ts),
        grid=(nb, s // ts),
        in_specs=[
            tile(D_MODEL),
            const((1, D_MODEL)),
            const((D_MODEL, FOX_COLS)),
            const((1, MIX_WIDTH)),
            const((1, MIX_WIDTH)),
            const((1, MEM_WIDTH)),
            const((1, F_PAD)),
            const((MXU_DIM, MXU_DIM)),
            const((ts, ts)),
        ],
        out_specs=[
            tile(MIX_WIDTH), tile(MIX_WIDTH),
            pl.BlockSpec((1, MIX_HEADS, VT_ROWS, ts), lambda b, i: (b, 0, 0, i)),
            tile(MEM_WIDTH), tile(F_PAD),
            pl.BlockSpec((1, 1, 1, F_PAD), lambda b, i: (b, i, 0, 0)),
        ],
        out_shape=[
            jax.ShapeDtypeStruct((nb, s, MIX_WIDTH), BF16),
            jax.ShapeDtypeStruct((nb, s, MIX_WIDTH), BF16),
            jax.ShapeDtypeStruct((nb, MIX_HEADS, VT_ROWS, s), BF16),
            jax.ShapeDtypeStruct((nb, s, MEM_WIDTH), BF16),
            jax.ShapeDtypeStruct((nb, s, F_PAD), BF16),
            jax.ShapeDtypeStruct((nb, s // ts, 1, F_PAD), F32),
        ],
        scratch_shapes=[pltpu.VMEM((1, F_PAD), F32), pltpu.VMEM((1, F_PAD), F32)],
        compiler_params=pltpu.CompilerParams(
            dimension_semantics=("arbitrary", "arbitrary"), vmem_limit_bytes=VMEM_LIMIT_BYTES),
        name="fox_proj",
    )(h, g, w, qg, kg, mqg, bf, bd, tri)


def _fox_attn_kernel(base_ref, q_ref, k_ref, e_ref, vt_ref, o_ref, qc_ref, m_ref, acc_ref,
                     s_ref, mc_ref, p_ref, al_ref, *, tq, n_tiles):
    b = pl.program_id(0)
    hp = pl.program_id(1)
    lane = lax.broadcasted_iota(jnp.int32, (tq, LANES), 1)

    def query_tile(qi, carry):
        q_start = pl.multiple_of(qi * tq, tq)
        qf = q_ref[0, pl.ds(q_start, tq), :].astype(F32)
        for a in range(2):
            h = 2 * hp + a
            own = (lane < HEAD_DIM) if a == 0 else (lane >= HEAD_DIM)
            sel = (jnp.bitwise_and(lane, E_STRIDE - 1) == h) & (lane < 3 * E_STRIDE)
            qc_ref[a] = jnp.concatenate(
                [jnp.where(own, qf, 0.0), jnp.where(sel, 1.0, 0.0)], axis=-1).astype(BF16)
        m_ref[...] = jnp.full(m_ref.shape, -jnp.inf, F32)
        acc_ref[...] = jnp.zeros(acc_ref.shape, F32)

        def scores(a, j, slot, masked):
            k_start = pl.multiple_of(j * tq, tq)
            kc = jnp.concatenate(
                [k_ref[0, pl.ds(k_start, tq), :], e_ref[0, pl.ds(k_start, tq), :]], axis=-1)
            raw = _dot_nt(kc, qc_ref[a])
            if masked:
                key = lax.broadcasted_iota(jnp.int32, (tq, tq), 0)
                qry = lax.broadcasted_iota(jnp.int32, (tq, tq), 1)
                raw = jnp.where(key <= qry, raw, -jnp.inf)
            s_ref[a, slot] = raw
            mc_ref[a, slot] = jnp.max(raw, axis=0, keepdims=True)

        def softmax(a, j, slot):
            h = 2 * hp + a
            delta = (base_ref[(b * n_tiles + qi) * E_STRIDE + h]
                     - base_ref[(b * n_tiles + j) * E_STRIDE + h])
            m_prev = m_ref[a]
            m_new = jnp.maximum(m_prev, mc_ref[a, slot] + delta)
            al_ref[a] = jnp.exp2(m_prev - m_new)
            p_ref[a] = jnp.exp2(s_ref[a, slot] - (m_new - delta)).astype(BF16)
            m_ref[a] = m_new

        def values(a, j):
            k_start = pl.multiple_of(j * tq, tq)
            acc_ref[a] = al_ref[a] * acc_ref[a] + _dot(vt_ref[0, a, :, pl.ds(k_start, tq)], p_ref[a])

        def round_(t, slot, issue_scores):
            prev = jnp.where(t == 1, qi, t - 2)
            values(0, prev)
            values(1, prev)
            if issue_scores:
                scores(0, t, 1 - slot, False)
                scores(1, t, 1 - slot, False)
            softmax(0, t - 1, slot)
            softmax(1, t - 1, slot)

        scores(0, qi, 0, True)
        scores(1, qi, 0, True)
        scores(0, 0, 1, False)
        scores(1, 0, 1, False)
        softmax(0, qi, 0)
        softmax(1, qi, 0)

        def round_pair(i, c):
            round_(2 * i + 1, 1, True)
            round_(2 * i + 2, 0, True)
            return c

        lax.fori_loop(0, (qi - 1) // 2, round_pair, 0)

        @pl.when(qi % 2 == 1)
        def _():
            round_(qi, 1, False)

        @pl.when((qi % 2 == 0) & (qi > 0))
        def _():
            round_(qi - 1, 1, True)
            round_(qi, 0, False)

        last = jnp.where(qi == 0, qi, qi - 1)
        values(0, last)
        values(1, last)
        out_t = jnp.concatenate(
            [acc_ref[a, :HEAD_DIM, :] * (1.0 / acc_ref[a, HEAD_DIM:HEAD_DIM + 1, :]) for a in range(2)],
            axis=0)
        o_ref[0, pl.ds(q_start, tq), :] = out_t.T.astype(BF16)
        return carry

    lax.fori_loop(0, n_tiles, query_tile, 0)


def _fox_attention(base, q, k, e, vt, tq):
    nb, s, _ = q.shape
    seq_block = lambda width: pl.BlockSpec((1, s, width), lambda b, p: (b, 0, p))
    return pl.pallas_call(
        functools.partial(_fox_attn_kernel, tq=tq, n_tiles=s // tq),
        grid=(nb, MIX_WIDTH // LANES),
        in_specs=[
            pl.BlockSpec(memory_space=pltpu.SMEM),
            seq_block(LANES),
            seq_block(LANES),
            pl.BlockSpec((1, s, F_PAD), lambda b, p: (b, 0, 0)),
            pl.BlockSpec((1, 2, VT_ROWS, s), lambda b, p: (b, p, 0, 0)),
        ],
        out_specs=seq_block(LANES),
        out_shape=jax.ShapeDtypeStruct((nb, s, MIX_WIDTH), BF16),
        scratch_shapes=[
            pltpu.VMEM((2, tq, 2 * LANES), BF16),
            pltpu.VMEM((2, 1, tq), F32),
            pltpu.VMEM((2, VT_ROWS, tq), F32),
            pltpu.VMEM((2, 2, tq, tq), F32),
            pltpu.VMEM((2, 2, 1, tq), F32),
            pltpu.VMEM((2, tq, tq), BF16),
            pltpu.VMEM((2, 1, tq), F32),
        ],
        compiler_params=pltpu.CompilerParams(
            dimension_semantics=("arbitrary", "arbitrary"), vmem_limit_bytes=VMEM_LIMIT_BYTES),
        name="fox_attention",
    )(base, q, k, e, vt)


def _fox_out_kernel(h_ref, tok_ref, qm_ref, kbd_ref, vbd_ref, wout_ref, o_ref):
    cross = _cross_attention(qm_ref[0], kbd_ref, vbd_ref)
    o_ref[0] = _mix_out(h_ref[0], tok_ref[0], cross, wout_ref)


def _fox_out(h, tok, qm, kbd, vbd, w_out, ts):
    nb, s, _ = h.shape
    tile = lambda width: pl.BlockSpec((1, ts, width), lambda b, i: (b, i, 0))
    per_batch = lambda shape: pl.BlockSpec((1,) + shape, lambda b, i: (b, 0, 0))
    return pl.pallas_call(
        _fox_out_kernel,
        grid=(nb, s // ts),
        in_specs=[
            tile(D_MODEL), tile(MIX_WIDTH), tile(MEM_WIDTH),
            per_batch((MEM_WIDTH, MEM_HEADS * N_MEM)),
            per_batch((MEM_HEADS * N_MEM, MEM_WIDTH)),
            pl.BlockSpec((D_MODEL, D_MODEL), lambda b, i: (0, 0), pipeline_mode=pl.Buffered(1)),
        ],
        out_specs=tile(D_MODEL),
        out_shape=jax.ShapeDtypeStruct(h.shape, F32),
        compiler_params=pltpu.CompilerParams(
            dimension_semantics=("arbitrary", "arbitrary"), vmem_limit_bytes=VMEM_LIMIT_BYTES),
        name="fox_out",
    )(h, tok, qm, kbd, vbd, w_out)


def _row(v):
    return v.reshape(1, -1).astype(F32)


def _tile_heads(g, heads):
    return jnp.tile(g.astype(F32), heads).reshape(1, -1)


def _gate_weights(w_rg, w_ig):
    def block_diag(w):
        groups = w.reshape(MIX_HEADS // HEADS_PER_MXU_TILE, HEADS_PER_MXU_TILE, HEAD_DIM, HEAD_DIM)
        eye = jnp.eye(HEADS_PER_MXU_TILE, dtype=w.dtype)
        bd = jnp.einsum('ghij,hk->ghikj', groups, eye)
        return bd.reshape(-1, MXU_DIM, MXU_DIM)
    return jnp.concatenate([block_diag(w_rg), block_diag(w_ig)], axis=-1).astype(BF16)


def _fox_weights(w):
    qkv = w[:, :3 * MIX_WIDTH]
    f = w[:, 3 * MIX_WIDTH:3 * MIX_WIDTH + MIX_HEADS]
    qm = w[:, 3 * MIX_WIDTH + MIX_HEADS:]
    f = jnp.pad(f, ((0, 0), (0, F_PAD - MIX_HEADS)))
    return jnp.concatenate([qkv, qm, f], axis=-1).astype(BF16)


def kernel(x, mem, mem_norm_g, mem_w_kv, mem_k_norm_g, ffn1_norm_g, ffn1_w_in, ffn1_w_out, mix_norm_g, mix_w_out, memq_norm_g, ffn2_norm_g, ffn2_w_in, ffn2_w_out, lru_w_in, lru_conv_w, lru_conv_b, lru_w_rg, lru_b_rg, lru_w_ig, lru_b_ig, lru_lambda, fox_w_in, fox_b_f, fox_q_norm_g, fox_k_norm_g):
    nb, s, d = x.shape
    assert d == D_MODEL and mem.shape[1:] == (N_MEM, D_MODEL)
    ts = min(ROW_TILE, s)
    tq = min(ATTN_TILE, s)
    assert s % ts == 0 and s % tq == 0 and (nb * s) % ts == 0

    head_id = jnp.arange(MXU_DIM) // HEAD_DIM
    bd = (head_id[:, None] == head_id[None, :]).astype(BF16)
    tri = (jnp.arange(ts)[:, None] >= jnp.arange(ts)[None, :]).astype(BF16)

    kbd, vbd = _memory_kv(mem, _row(mem_norm_g), mem_w_kv.astype(BF16),
                          _tile_heads(mem_k_norm_g, MEM_HEADS), bd)

    def ffn(h, g, w_in, w_out):
        out = _ffn(h.reshape(nb * s, d), _row(g), w_in.astype(BF16), w_out.astype(BF16),
                   min(FFN_ROW_TILE, nb * s))
        return out.reshape(nb, s, d)

    h = x
    h = ffn(h, ffn1_norm_g[0], ffn1_w_in[0], ffn1_w_out[0])
    h = _lru_mix(h, _row(mix_norm_g[0]), lru_w_in[0].astype(BF16), lru_conv_w[0].astype(F32),
                 _row(lru_conv_b[0]), _gate_weights(lru_w_rg[0], lru_w_ig[0]), _row(lru_b_rg[0]),
                 _row(lru_b_ig[0]), _row(lru_lambda[0]), _tile_heads(memq_norm_g[0], MEM_HEADS),
                 bd, kbd, vbd, mix_w_out[0].astype(BF16), min(LRU_ROW_TILE, s))
    h = ffn(h, ffn2_norm_g[0], ffn2_w_in[0], ffn2_w_out[0])
    h = ffn(h, ffn1_norm_g[1], ffn1_w_in[1], ffn1_w_out[1])
    bf = jnp.pad(fox_b_f[0].astype(F32), (0, F_PAD - MIX_HEADS)).reshape(1, F_PAD)
    q, k, vt, qm, e, base = _fox_proj(h, _row(mix_norm_g[1]), _fox_weights(fox_w_in[0]),
                                      _tile_heads(fox_q_norm_g[0], MIX_HEADS),
                                      _tile_heads(fox_k_norm_g[0], MIX_HEADS),
                                      _tile_heads(memq_norm_g[1], MEM_HEADS), bf, bd, tri, ts, tq)
    base = base[:, ::tq // ts, 0, :E_STRIDE].reshape(-1)
    tok = _fox_attention(base, q, k, e, vt, tq)
    h = _fox_out(h, tok, qm, kbd, vbd, mix_w_out[1].astype(BF16), ts)
    h = ffn(h, ffn2_norm_g[1], ffn2_w_in[1], ffn2_w_out[1])
    return h
```

```python
import functools

import jax
import jax.numpy as jnp
from jax import lax
from jax.experimental import pallas as pl
from jax.experimental.pallas import tpu as pltpu

F32 = jnp.float32
BF16 = jnp.bfloat16

D_MODEL = 1024
HEAD_DIM = 64
MIX_HEADS = 12
MIX_WIDTH = MIX_HEADS * HEAD_DIM
MEM_HEADS = 4
MEM_WIDTH = MEM_HEADS * HEAD_DIM
N_MEM = 256
D_FF = 2816
CONV_WIDTH = 4
LRU_C = 8.0
NORM_EPS = 1e-6
SCORE_SCALE = HEAD_DIM ** -0.5
LOG2E = 1.4426950408889634
VT_ROWS = 80

LANES = 128
SUBLANES = 8
MXU_DIM = 256
HEADS_PER_MXU_TILE = MXU_DIM // HEAD_DIM
F_PAD = LANES
FOX_COLS = 3 * MIX_WIDTH + MEM_WIDTH + F_PAD
E_STRIDE = 16
VMEM_LIMIT_BYTES = 56 * 1024 * 1024

ROW_TILE = 256
FFN_ROW_TILE = 512
LRU_ROW_TILE = 512
ATTN_TILE = 512


def _dot(a, b):
    return jnp.dot(a, b, preferred_element_type=F32)


def _dot_nt(a, b):
    return lax.dot_general(a, b, (((1,), (1,)), ((), ())), preferred_element_type=F32)


def _rms(x, g):
    ms = jnp.mean(x * x, axis=-1, keepdims=True)
    return x * lax.rsqrt(ms + NORM_EPS) * g


def _head_rms(x, g, bd_ref):
    outs = []
    for c in range(x.shape[1] // MXU_DIM):
        xc = x[:, c * MXU_DIM:(c + 1) * MXU_DIM]
        x2 = xc * xc
        hi = x2.astype(BF16)
        lo = (x2 - hi.astype(F32)).astype(BF16)
        ms = (_dot(hi, bd_ref[...]) + _dot(lo, bd_ref[...])) * (1.0 / HEAD_DIM)
        outs.append(xc * lax.rsqrt(ms + NORM_EPS))
    y = outs[0] if len(outs) == 1 else jnp.concatenate(outs, axis=-1)
    return y * g


def _softplus(x):
    return jnp.maximum(x, 0.0) + jnp.log1p(jnp.exp(-jnp.abs(x)))


def _memkv_kernel(mem_ref, g_ref, w_ref, kg_ref, bd_ref, kbd_ref, vbd_ref):
    xn = _rms(mem_ref[0], g_ref[...]).astype(BF16)
    kv = _dot(xn, w_ref[...])
    k = _head_rms(kv[:, :MEM_WIDTH], kg_ref[...], bd_ref)
    v = kv[:, MEM_WIDTH:]
    kt = k.T
    shape = (MEM_WIDTH, N_MEM)
    row_head = lax.shift_right_logical(lax.broadcasted_iota(jnp.int32, shape, 0), 6)
    col_head = lax.shift_right_logical(lax.broadcasted_iota(jnp.int32, (N_MEM, MEM_WIDTH), 1), 6)
    for h in range(MEM_HEADS):
        kbd_ref[0, :, h * N_MEM:(h + 1) * N_MEM] = jnp.where(row_head == h, kt, 0.0).astype(BF16)
        vbd_ref[0, h * N_MEM:(h + 1) * N_MEM, :] = jnp.where(col_head == h, v, 0.0).astype(BF16)


def _memory_kv(mem, g, w_kv, kg, bd):
    nb = mem.shape[0]
    const = lambda shape: pl.BlockSpec(shape, lambda b: (0,) * len(shape))
    return pl.pallas_call(
        _memkv_kernel,
        grid=(nb,),
        in_specs=[
            pl.BlockSpec((1, N_MEM, D_MODEL), lambda b: (b, 0, 0)),
            const((1, D_MODEL)),
            const((D_MODEL, 2 * MEM_WIDTH)),
            const((1, MEM_WIDTH)),
            const((MXU_DIM, MXU_DIM)),
        ],
        out_specs=[
            pl.BlockSpec((1, MEM_WIDTH, MEM_HEADS * N_MEM), lambda b: (b, 0, 0)),
            pl.BlockSpec((1, MEM_HEADS * N_MEM, MEM_WIDTH), lambda b: (b, 0, 0)),
        ],
        out_shape=[
            jax.ShapeDtypeStruct((nb, MEM_WIDTH, MEM_HEADS * N_MEM), BF16),
            jax.ShapeDtypeStruct((nb, MEM_HEADS * N_MEM, MEM_WIDTH), BF16),
        ],
        compiler_params=pltpu.CompilerParams(
            dimension_semantics=("arbitrary",), vmem_limit_bytes=VMEM_LIMIT_BYTES),
        name="memory_kv",
    )(mem, g, w_kv, kg, bd)


def _ffn_kernel(h_ref, g_ref, win_ref, wout_ref, o_ref):
    half = h_ref.shape[0] // 2
    rows = [slice(k * half, (k + 1) * half) for k in range(2)]
    gu = [_dot(_rms(h_ref[r, :], g_ref[...]).astype(BF16), win_ref[...]) for r in rows]
    for r, gu_k in zip(rows, gu):
        act = (jax.nn.silu(gu_k[:, :D_FF]) * gu_k[:, D_FF:]).astype(BF16)
        o_ref[r, :] = h_ref[r, :] + 0.5 * _dot(act, wout_ref[...])


def _ffn(h, g, w_in, w_out, tm):
    t = h.shape[0]
    resident = lambda shape: pl.BlockSpec(shape, lambda i: (0, 0), pipeline_mode=pl.Buffered(1))
    return pl.pallas_call(
        _ffn_kernel,
        grid=(t // tm,),
        in_specs=[
            pl.BlockSpec((tm, D_MODEL), lambda i: (i, 0)),
            resident((1, D_MODEL)),
            resident((D_MODEL, 2 * D_FF)),
            resident((D_FF, D_MODEL)),
        ],
        out_specs=pl.BlockSpec((tm, D_MODEL), lambda i: (i, 0)),
        out_shape=jax.ShapeDtypeStruct((t, D_MODEL), F32),
        compiler_params=pltpu.CompilerParams(
            dimension_semantics=("arbitrary",), vmem_limit_bytes=VMEM_LIMIT_BYTES),
        name="swiglu_ffn",
    )(h, g, w_in, w_out)


def _cross_probs(s):
    probs = []
    for h in range(MEM_HEADS):
        sh = s[:, h * N_MEM:(h + 1) * N_MEM]
        e = jnp.exp(sh - jnp.max(sh, axis=-1, keepdims=True))
        inv = 1.0 / jnp.sum(e, axis=-1, keepdims=True)
        probs.append((e * inv).astype(BF16))
    return jnp.concatenate(probs, axis=-1)


def _cross_attention(qm, kbd_ref, vbd_ref):
    return _dot(_cross_probs(_dot(qm, kbd_ref[0])), vbd_ref[0])


def _mix_out(h, tok, cross, wout_ref):
    cat = jnp.concatenate([tok.astype(BF16), cross.astype(BF16)], axis=-1)
    return h + _dot(cat, wout_ref[...])


def _lru_kernel(h_ref, g_ref, win_ref, cw_ref, cb_ref, wg_ref, brg_ref, big_ref, lam_ref,
                mqg_ref, bd_ref, kbd_ref, vbd_ref, wout_ref, o_ref,
                xs_ref, gq_ref, a_ref, b_ref, hprev_ref, *, ts):
    half = ts // 2

    @pl.when(pl.program_id(1) == 0)
    def _():
        xs_ref[0:SUBLANES, :] = jnp.zeros((SUBLANES, MIX_WIDTH), F32)
        hprev_ref[...] = jnp.zeros((SUBLANES, MIX_WIDTH), F32)

    def project(k):
        r0 = k * half
        hn = _rms(h_ref[0, r0:r0 + half, :], g_ref[...]).astype(BF16)
        proj = _dot(hn, win_ref[...])
        xs_ref[SUBLANES + r0:SUBLANES + r0 + half, :] = proj[:, :MIX_WIDTH]
        gq_ref[k] = proj[:, MIX_WIDTH:]

    def gates(k):
        r0 = k * half
        xe = xs_ref[r0:r0 + half + SUBLANES, :]
        acc = xe * cw_ref[0:1, :]
        for tap in range(1, CONV_WIDTH):
            acc = pltpu.roll(acc, 1, 0) + xe * cw_ref[tap:tap + 1, :]
        xc = cb_ref[...] + acc[SUBLANES:, :]
        xcb = xc.astype(BF16)
        r_parts, i_parts = [], []
        for c in range(MIX_WIDTH // MXU_DIM):
            gg = _dot(xcb[:, c * MXU_DIM:(c + 1) * MXU_DIM], wg_ref[c])
            r_parts.append(gg[:, :MXU_DIM])
            i_parts.append(gg[:, MXU_DIM:])
        r = jax.nn.sigmoid(jnp.concatenate(r_parts, axis=-1) + brg_ref[...])
        gi = jax.nn.sigmoid(jnp.concatenate(i_parts, axis=-1) + big_ref[...])
        log_a = -LRU_C * r * _softplus(-lam_ref[...])
        t = jnp.tanh(-log_a)
        a = jnp.exp(log_a)
        t2 = 2.0 * t
        root = jnp.where(t > 0.0, t2 * lax.rsqrt(t2 * (1.0 + t)), 0.0)
        bx = root * (gi * xc)
        for lt in range(MIX_WIDTH // LANES):
            a_ref[lt, r0:r0 + half, :] = a[:, lt * LANES:(lt + 1) * LANES]
            b_ref[lt, r0:r0 + half, :] = bx[:, lt * LANES:(lt + 1) * LANES]

    row = lax.broadcasted_iota(jnp.int32, (SUBLANES, LANES), 0)

    def scan(k, hprev):
        blk = SUBLANES * SUBLANES
        hprev = list(hprev)
        for i in range(half // blk):
            base = k * half + i * blk
            for lt in range(MIX_WIDTH // LANES):
                a = [a_ref[lt, pl.ds(base + j, SUBLANES, stride=SUBLANES), :] for j in range(SUBLANES)]
                b = [b_ref[lt, pl.ds(base + j, SUBLANES, stride=SUBLANES), :] for j in range(SUBLANES)]
                hloc, prod = [b[0]], [a[0]]
                for j in range(1, SUBLANES):
                    hloc.append(a[j] * hloc[-1] + b[j])
                    prod.append(a[j] * prod[-1])
                ea, eb = prod[-1], hloc[-1]
                for sh in (1, 2, 4):
                    keep = row >= sh
                    a_sh = jnp.where(keep, pltpu.roll(ea, sh, 0), 1.0)
                    b_sh = jnp.where(keep, pltpu.roll(eb, sh, 0), 0.0)
                    eb = eb + ea * b_sh
                    ea = ea * a_sh
                ends = eb + ea * hprev[lt]
                incoming = jnp.where(row >= 1, pltpu.roll(ends, 1, 0), hprev[lt])
                for j in range(SUBLANES):
                    b_ref[lt, pl.ds(base + j, SUBLANES, stride=SUBLANES), :] = hloc[j] + prod[j] * incoming
                hprev[lt] = jnp.broadcast_to(ends[SUBLANES - 1:SUBLANES, :], (SUBLANES, LANES))
        return hprev

    def finish(k):
        r0 = k * half
        gq = gq_ref[k]
        hs = jnp.concatenate([b_ref[lt, r0:r0 + half, :] for lt in range(MIX_WIDTH // LANES)], axis=-1)
        tok = hs * jax.nn.gelu(gq[:, :MIX_WIDTH])
        qm = (_head_rms(gq[:, MIX_WIDTH:], mqg_ref[...], bd_ref) * SCORE_SCALE).astype(BF16)
        cross = _cross_attention(qm, kbd_ref, vbd_ref)
        o_ref[0, r0:r0 + half, :] = _mix_out(h_ref[0, r0:r0 + half, :], tok, cross, wout_ref)

    project(0)
    gates(0)
    project(1)
    gates(1)
    xs_ref[0:SUBLANES, :] = xs_ref[ts:ts + SUBLANES, :]
    hmid = scan(0, [hprev_ref[:, lt * LANES:(lt + 1) * LANES] for lt in range(MIX_WIDTH // LANES)])
    finish(0)
    hprev_ref[...] = jnp.concatenate(scan(1, hmid), axis=-1)
    finish(1)


def _lru_mix(h, g, w_in, conv_w, conv_b, wg, b_rg, b_ig, lam, mqg, bd, kbd, vbd, w_out, ts):
    nb, s, _ = h.shape
    const = lambda shape: pl.BlockSpec(shape, lambda b, i: (0,) * len(shape), pipeline_mode=pl.Buffered(1))
    per_batch = lambda shape: pl.BlockSpec((1,) + shape, lambda b, i: (b, 0, 0))
    return pl.pallas_call(
        functools.partial(_lru_kernel, ts=ts),
        grid=(nb, s // ts),
        in_specs=[
            pl.BlockSpec((1, ts, D_MODEL), lambda b, i: (b, i, 0)),
            const((1, D_MODEL)),
            const((D_MODEL, 2 * MIX_WIDTH + MEM_WIDTH)),
            const((CONV_WIDTH, MIX_WIDTH)),
            const((1, MIX_WIDTH)),
            const((MIX_WIDTH // MXU_DIM, MXU_DIM, 2 * MXU_DIM)),
            const((1, MIX_WIDTH)),
            const((1, MIX_WIDTH)),
            const((1, MIX_WIDTH)),
            const((1, MEM_WIDTH)),
            const((MXU_DIM, MXU_DIM)),
            per_batch((MEM_WIDTH, MEM_HEADS * N_MEM)),
            per_batch((MEM_HEADS * N_MEM, MEM_WIDTH)),
            const((D_MODEL, D_MODEL)),
        ],
        out_specs=pl.BlockSpec((1, ts, D_MODEL), lambda b, i: (b, i, 0)),
        out_shape=jax.ShapeDtypeStruct(h.shape, F32),
        scratch_shapes=[
            pltpu.VMEM((ts + SUBLANES, MIX_WIDTH), F32),
            pltpu.VMEM((2, ts // 2, MIX_WIDTH + MEM_WIDTH), F32),
            pltpu.VMEM((MIX_WIDTH // LANES, ts, LANES), F32),
            pltpu.VMEM((MIX_WIDTH // LANES, ts, LANES), F32),
            pltpu.VMEM((SUBLANES, MIX_WIDTH), F32),
        ],
        compiler_params=pltpu.CompilerParams(
            dimension_semantics=("arbitrary", "arbitrary"), vmem_limit_bytes=VMEM_LIMIT_BYTES),
        name="rglru_mix",
    )(h, g, w_in, conv_w, conv_b, wg, b_rg, b_ig, lam, mqg, bd, kbd, vbd, w_out)


def _split3(x):
    p0 = x.astype(BF16)
    r1 = x - p0.astype(F32)
    p1 = r1.astype(BF16)
    p2 = (r1 - p1.astype(F32)).astype(BF16)
    return p0, p1, p2


def _fox_proj_kernel(h_ref, g_ref, w_ref, qg_ref, kg_ref, mqg_ref, bf_ref, bd_ref, tri_ref,
                     q_ref, k_ref, vt_ref, qm_ref, e_ref, base_ref, rel_ref, abs_ref, *, tiles_per_attn):
    t = pl.program_id(1)

    @pl.when(t == 0)
    def _():
        abs_ref[...] = jnp.zeros((1, F_PAD), F32)

    @pl.when(t % tiles_per_attn == 0)
    def _():
        rel_ref[...] = jnp.zeros((1, F_PAD), F32)

    hn = _rms(h_ref[0], g_ref[...]).astype(BF16)
    proj = _dot(hn, w_ref[...])
    w = MIX_WIDTH
    q_ref[0] = (_head_rms(proj[:, :w], qg_ref[...], bd_ref) * (SCORE_SCALE * LOG2E)).astype(BF16)
    k_ref[0] = _head_rms(proj[:, w:2 * w], kg_ref[...], bd_ref).astype(BF16)
    vt = proj[:, 2 * w:3 * w].T.astype(BF16)
    pad_rows = VT_ROWS - HEAD_DIM
    ones_row = jnp.where(lax.broadcasted_iota(jnp.int32, (pad_rows, vt.shape[1]), 0) == 0, 1.0, 0.0)
    for hd in range(MIX_HEADS):
        vt_ref[0, hd, :HEAD_DIM, :] = vt[hd * HEAD_DIM:(hd + 1) * HEAD_DIM, :]
        vt_ref[0, hd, HEAD_DIM:, :] = ones_row.astype(BF16)
    qm = proj[:, 3 * w:3 * w + MEM_WIDTH]
    qm_ref[0] = (_head_rms(qm, mqg_ref[...], bd_ref) * SCORE_SCALE).astype(BF16)

    log_f = -_softplus(-(proj[:, 3 * w + MEM_WIDTH:] + bf_ref[...]))
    tri = tri_ref[...]
    rel = rel_ref[...] + sum(_dot(tri, p) for p in _split3(log_f))
    last = rel[rel.shape[0] - 1:, :]
    lane = lax.broadcasted_iota(jnp.int32, rel.shape, 1)
    e0, e1, e2 = _split3(jnp.where(lane < MIX_HEADS, -LOG2E * rel, 0.0))
    e_ref[0] = (e0.astype(F32) + pltpu.roll(e1.astype(F32), E_STRIDE, 1)
                + pltpu.roll(e2.astype(F32), 2 * E_STRIDE, 1)).astype(BF16)
    base_ref[0, 0] = LOG2E * abs_ref[...]
    rel_ref[...] = last

    @pl.when(t % tiles_per_attn == tiles_per_attn - 1)
    def _():
        abs_ref[...] = abs_ref[...] + last


def _fox_proj(h, g, w, qg, kg, mqg, bf, bd, tri, ts, tq):
    nb, s, _ = h.shape
    const = lambda shape: pl.BlockSpec(shape, lambda b, i: (0,) * len(shape), pipeline_mode=pl.Buffered(1))
    tile = lambda width: pl.BlockSpec((1, ts, width), lambda b, i: (b, i, 0))
    return pl.pallas_call(
        functools.partial(_fox_proj_kernel, tiles_per_attn=tq // ts),
        grid=(nb, s // ts),
        in_specs=[
            tile(D_MODEL),
            const((1, D_MODEL)),
            const((D_MODEL, FOX_COLS)),
            const((1, MIX_WIDTH)),
            const((1, MIX_WIDTH)),
            const((1, MEM_WIDTH)),
            const((1, F_PAD)),
            const((MXU_DIM, MXU_DIM)),
            const((ts, ts)),
        ],
        out_specs=[
            tile(MIX_WIDTH), tile(MIX_WIDTH),
            pl.BlockSpec((1, MIX_HEADS, VT_ROWS, ts), lambda b, i: (b, 0, 0, i)),
            tile(MEM_WIDTH), tile(F_PAD),
            pl.BlockSpec((1, 1, 1, F_PAD), lambda b, i: (b, i, 0, 0)),
        ],
        out_shape=[
            jax.ShapeDtypeStruct((nb, s, MIX_WIDTH), BF16),
            jax.ShapeDtypeStruct((nb, s, MIX_WIDTH), BF16),
            jax.ShapeDtypeStruct((nb, MIX_HEADS, VT_ROWS, s), BF16),
            jax.ShapeDtypeStruct((nb, s, MEM_WIDTH), BF16),
            jax.ShapeDtypeStruct((nb, s, F_PAD), BF16),
            jax.ShapeDtypeStruct((nb, s // ts, 1, F_PAD), F32),
        ],
        scratch_shapes=[pltpu.VMEM((1, F_PAD), F32), pltpu.VMEM((1, F_PAD), F32)],
        compiler_params=pltpu.CompilerParams(
            dimension_semantics=("arbitrary", "arbitrary"), vmem_limit_bytes=VMEM_LIMIT_BYTES),
        name="fox_proj",
    )(h, g, w, qg, kg, mqg, bf, bd, tri)


def _fox_attn_kernel(base_ref, q_ref, k_ref, e_ref, vt_ref, o_ref, qc_ref, m_ref, acc_ref,
                     s_ref, mc_ref, p_ref, al_ref, *, tq, n_tiles):
    b = pl.program_id(0)
    hp = pl.program_id(1)
    lane = lax.broadcasted_iota(jnp.int32, (tq, LANES), 1)

    def query_tile(qi, carry):
        q_start = pl.multiple_of(qi * tq, tq)
        qf = q_ref[0, pl.ds(q_start, tq), :].astype(F32)
        for a in range(2):
            h = 2 * hp + a
            own = (lane < HEAD_DIM) if a == 0 else (lane >= HEAD_DIM)
            sel = (jnp.bitwise_and(lane, E_STRIDE - 1) == h) & (lane < 3 * E_STRIDE)
            qc_ref[a] = jnp.concatenate(
                [jnp.where(own, qf, 0.0), jnp.where(sel, 1.0, 0.0)], axis=-1).astype(BF16)
        m_ref[...] = jnp.full(m_ref.shape, -jnp.inf, F32)
        acc_ref[...] = jnp.zeros(acc_ref.shape, F32)

        def scores(a, j, slot, masked):
            k_start = pl.multiple_of(j * tq, tq)
            kc = jnp.concatenate(
                [k_ref[0, pl.ds(k_start, tq), :], e_ref[0, pl.ds(k_start, tq), :]], axis=-1)
            raw = _dot_nt(kc, qc_ref[a])
            if masked:
                key = lax.broadcasted_iota(jnp.int32, (tq, tq), 0)
                qry = lax.broadcasted_iota(jnp.int32, (tq, tq), 1)
                raw = jnp.where(key <= qry, raw, -jnp.inf)
            s_ref[a, slot] = raw
            mc_ref[a, slot] = jnp.max(raw, axis=0, keepdims=True)

        def softmax(a, j, slot):
            h = 2 * hp + a
            delta = (base_ref[(b * n_tiles + qi) * E_STRIDE + h]
                     - base_ref[(b * n_tiles + j) * E_STRIDE + h])
            m_prev = m_ref[a]
            m_new = jnp.maximum(m_prev, mc_ref[a, slot] + delta)
            al_ref[a] = jnp.exp2(m_prev - m_new)
            p_ref[a] = jnp.exp2(s_ref[a, slot] - (m_new - delta)).astype(BF16)
            m_ref[a] = m_new

        def values(a, j):
            k_start = pl.multiple_of(j * tq, tq)
            acc_ref[a] = al_ref[a] * acc_ref[a] + _dot(vt_ref[0, a, :, pl.ds(k_start, tq)], p_ref[a])

        def round_(t, slot, issue_scores):
            prev = jnp.where(t == 1, qi, t - 2)
            values(0, prev)
            values(1, prev)
            if issue_scores:
                scores(0, t, 1 - slot, False)
                scores(1, t, 1 - slot, False)
            softmax(0, t - 1, slot)
            softmax(1, t - 1, slot)

        scores(0, qi, 0, True)
        scores(1, qi, 0, True)
        scores(0, 0, 1, False)
        scores(1, 0, 1, False)
        softmax(0, qi, 0)
        softmax(1, qi, 0)

        def round_pair(i, c):
            round_(2 * i + 1, 1, True)
            round_(2 * i + 2, 0, True)
            return c

        lax.fori_loop(0, (qi - 1) // 2, round_pair, 0)

        @pl.when(qi % 2 == 1)
        def _():
            round_(qi, 1, False)

        @pl.when((qi % 2 == 0) & (qi > 0))
        def _():
            round_(qi - 1, 1, True)
            round_(qi, 0, False)

        last = jnp.where(qi == 0, qi, qi - 1)
        values(0, last)
        values(1, last)
        out_t = jnp.concatenate(
            [acc_ref[a, :HEAD_DIM, :] * (1.0 / acc_ref[a, HEAD_DIM:HEAD_DIM + 1, :]) for a in range(2)],
            axis=0)
        o_ref[0, pl.ds(q_start, tq), :] = out_t.T.astype(BF16)
        return carry

    lax.fori_loop(0, n_tiles, query_tile, 0)


def _fox_attention(base, q, k, e, vt, tq):
    nb, s, _ = q.shape
    seq_block = lambda width: pl.BlockSpec((1, s, width), lambda b, p: (b, 0, p))
    return pl.pallas_call(
        functools.partial(_fox_attn_kernel, tq=tq, n_tiles=s // tq),
        grid=(nb, MIX_WIDTH // LANES),
        in_specs=[
            pl.BlockSpec(memory_space=pltpu.SMEM),
            seq_block(LANES),
            seq_block(LANES),
            pl.BlockSpec((1, s, F_PAD), lambda b, p: (b, 0, 0)),
            pl.BlockSpec((1, 2, VT_ROWS, s), lambda b, p: (b, p, 0, 0)),
        ],
        out_specs=seq_block(LANES),
        out_shape=jax.ShapeDtypeStruct((nb, s, MIX_WIDTH), BF16),
        scratch_shapes=[
            pltpu.VMEM((2, tq, 2 * LANES), BF16),
            pltpu.VMEM((2, 1, tq), F32),
            pltpu.VMEM((2, VT_ROWS, tq), F32),
            pltpu.VMEM((2, 2, tq, tq), F32),
            pltpu.VMEM((2, 2, 1, tq), F32),
            pltpu.VMEM((2, tq, tq), BF16),
            pltpu.VMEM((2, 1, tq), F32),
        ],
        compiler_params=pltpu.CompilerParams(
            dimension_semantics=("arbitrary", "arbitrary"), vmem_limit_bytes=VMEM_LIMIT_BYTES),
        name="fox_attention",
    )(base, q, k, e, vt)


def _fox_out_kernel(h_ref, tok_ref, qm_ref, kbd_ref, vbd_ref, wout_ref, o_ref):
    half = h_ref.shape[1] // 2
    rows = [slice(k * half, (k + 1) * half) for k in range(2)]
    s = [_dot(qm_ref[0, r, :], kbd_ref[0]) for r in rows]
    cross = [None, None]
    cross[0] = _dot(_cross_probs(s[0]), vbd_ref[0])
    p1 = _cross_probs(s[1])
    o_ref[0, rows[0], :] = _mix_out(h_ref[0, rows[0], :], tok_ref[0, rows[0], :], cross[0], wout_ref)
    cross[1] = _dot(p1, vbd_ref[0])
    o_ref[0, rows[1], :] = _mix_out(h_ref[0, rows[1], :], tok_ref[0, rows[1], :], cross[1], wout_ref)


def _fox_out(h, tok, qm, kbd, vbd, w_out, ts):
    nb, s, _ = h.shape
    tile = lambda width: pl.BlockSpec((1, ts, width), lambda b, i: (b, i, 0))
    per_batch = lambda shape: pl.BlockSpec((1,) + shape, lambda b, i: (b, 0, 0))
    return pl.pallas_call(
        _fox_out_kernel,
        grid=(nb, s // ts),
        in_specs=[
            tile(D_MODEL), tile(MIX_WIDTH), tile(MEM_WIDTH),
            per_batch((MEM_WIDTH, MEM_HEADS * N_MEM)),
            per_batch((MEM_HEADS * N_MEM, MEM_WIDTH)),
            pl.BlockSpec((D_MODEL, D_MODEL), lambda b, i: (0, 0), pipeline_mode=pl.Buffered(1)),
        ],
        out_specs=tile(D_MODEL),
        out_shape=jax.ShapeDtypeStruct(h.shape, F32),
        compiler_params=pltpu.CompilerParams(
            dimension_semantics=("arbitrary", "arbitrary"), vmem_limit_bytes=VMEM_LIMIT_BYTES),
        name="fox_out",
    )(h, tok, qm, kbd, vbd, w_out)


def _row(v):
    return v.reshape(1, -1).astype(F32)


def _tile_heads(g, heads):
    return jnp.tile(g.astype(F32), heads).reshape(1, -1)


def _gate_weights(w_rg, w_ig):
    def block_diag(w):
        groups = w.reshape(MIX_HEADS // HEADS_PER_MXU_TILE, HEADS_PER_MXU_TILE, HEAD_DIM, HEAD_DIM)
        eye = jnp.eye(HEADS_PER_MXU_TILE, dtype=w.dtype)
        bd = jnp.einsum('ghij,hk->ghikj', groups, eye)
        return bd.reshape(-1, MXU_DIM, MXU_DIM)
    return jnp.concatenate([block_diag(w_rg), block_diag(w_ig)], axis=-1).astype(BF16)


def _fox_weights(w):
    qkv = w[:, :3 * MIX_WIDTH]
    f = w[:, 3 * MIX_WIDTH:3 * MIX_WIDTH + MIX_HEADS]
    qm = w[:, 3 * MIX_WIDTH + MIX_HEADS:]
    f = jnp.pad(f, ((0, 0), (0, F_PAD - MIX_HEADS)))
    return jnp.concatenate([qkv, qm, f], axis=-1).astype(BF16)


def kernel(x, mem, mem_norm_g, mem_w_kv, mem_k_norm_g, ffn1_norm_g, ffn1_w_in, ffn1_w_out, mix_norm_g, mix_w_out, memq_norm_g, ffn2_norm_g, ffn2_w_in, ffn2_w_out, lru_w_in, lru_conv_w, lru_conv_b, lru_w_rg, lru_b_rg, lru_w_ig, lru_b_ig, lru_lambda, fox_w_in, fox_b_f, fox_q_norm_g, fox_k_norm_g):
    nb, s, d = x.shape
    assert d == D_MODEL and mem.shape[1:] == (N_MEM, D_MODEL)
    ts = min(ROW_TILE, s)
    tq = min(ATTN_TILE, s)
    assert s % ts == 0 and s % tq == 0 and (nb * s) % ts == 0

    head_id = jnp.arange(MXU_DIM) // HEAD_DIM
    bd = (head_id[:, None] == head_id[None, :]).astype(BF16)
    tri = (jnp.arange(ts)[:, None] >= jnp.arange(ts)[None, :]).astype(BF16)

    kbd, vbd = _memory_kv(mem, _row(mem_norm_g), mem_w_kv.astype(BF16),
                          _tile_heads(mem_k_norm_g, MEM_HEADS), bd)

    def ffn(h, g, w_in, w_out):
        out = _ffn(h.reshape(nb * s, d), _row(g), w_in.astype(BF16), w_out.astype(BF16),
                   min(FFN_ROW_TILE, nb * s))
        return out.reshape(nb, s, d)

    h = x
    h = ffn(h, ffn1_norm_g[0], ffn1_w_in[0], ffn1_w_out[0])
    h = _lru_mix(h, _row(mix_norm_g[0]), lru_w_in[0].astype(BF16), lru_conv_w[0].astype(F32),
                 _row(lru_conv_b[0]), _gate_weights(lru_w_rg[0], lru_w_ig[0]), _row(lru_b_rg[0]),
                 _row(lru_b_ig[0]), _row(lru_lambda[0]), _tile_heads(memq_norm_g[0], MEM_HEADS),
                 bd, kbd, vbd, mix_w_out[0].astype(BF16), min(LRU_ROW_TILE, s))
    h = ffn(h, ffn2_norm_g[0], ffn2_w_in[0], ffn2_w_out[0])
    h = ffn(h, ffn1_norm_g[1], ffn1_w_in[1], ffn1_w_out[1])
    bf = jnp.pad(fox_b_f[0].astype(F32), (0, F_PAD - MIX_HEADS)).reshape(1, F_PAD)
    q, k, vt, qm, e, base = _fox_proj(h, _row(mix_norm_g[1]), _fox_weights(fox_w_in[0]),
                                      _tile_heads(fox_q_norm_g[0], MIX_HEADS),
                                      _tile_heads(fox_k_norm_g[0], MIX_HEADS),
                                      _tile_heads(memq_norm_g[1], MEM_HEADS), bf, bd, tri, ts, tq)
    base = base[:, ::tq // ts, 0, :E_STRIDE].reshape(-1)
    tok = _fox_attention(base, q, k, e, vt, tq)
    h = _fox_out(h, tok, qm, kbd, vbd, mix_w_out[1].astype(BF16), min(LRU_ROW_TILE, s))
    h = ffn(h, ffn2_norm_g[1], ffn2_w_in[1], ffn2_w_out[1])
    return h
```

```python
import functools

import jax
import jax.numpy as jnp
from jax import lax
from jax.experimental import pallas as pl
from jax.experimental.pallas import tpu as pltpu

F32 = jnp.float32
BF16 = jnp.bfloat16

D_MODEL = 1024
HEAD_DIM = 64
MIX_HEADS = 12
MIX_WIDTH = MIX_HEADS * HEAD_DIM
MEM_HEADS = 4
MEM_WIDTH = MEM_HEADS * HEAD_DIM
N_MEM = 256
D_FF = 2816
CONV_WIDTH = 4
LRU_C = 8.0
NORM_EPS = 1e-6
SCORE_SCALE = HEAD_DIM ** -0.5
LOG2E = 1.4426950408889634
VT_ROWS = 80

LANES = 128
SUBLANES = 8
MXU_DIM = 256
HEADS_PER_MXU_TILE = MXU_DIM // HEAD_DIM
F_PAD = LANES
FOX_COLS = 3 * MIX_WIDTH + MEM_WIDTH + F_PAD
E_STRIDE = 16
VMEM_LIMIT_BYTES = 56 * 1024 * 1024

ROW_TILE = 256
FFN_ROW_TILE = 512
LRU_ROW_TILE = 512
LRU_PARTS = 2
ATTN_TILE = 512


def _dot(a, b):
    return jnp.dot(a, b, preferred_element_type=F32)


def _dot_nt(a, b):
    return lax.dot_general(a, b, (((1,), (1,)), ((), ())), preferred_element_type=F32)


def _rms(x, g):
    ms = jnp.mean(x * x, axis=-1, keepdims=True)
    return x * lax.rsqrt(ms + NORM_EPS) * g


def _head_rms(x, g, bd_ref):
    outs = []
    for c in range(x.shape[1] // MXU_DIM):
        xc = x[:, c * MXU_DIM:(c + 1) * MXU_DIM]
        x2 = xc * xc
        hi = x2.astype(BF16)
        lo = (x2 - hi.astype(F32)).astype(BF16)
        ms = (_dot(hi, bd_ref[...]) + _dot(lo, bd_ref[...])) * (1.0 / HEAD_DIM)
        outs.append(xc * lax.rsqrt(ms + NORM_EPS))
    y = outs[0] if len(outs) == 1 else jnp.concatenate(outs, axis=-1)
    return y * g


def _softplus(x):
    return jnp.maximum(x, 0.0) + jnp.log1p(jnp.exp(-jnp.abs(x)))


def _memkv_kernel(mem_ref, g_ref, w_ref, kg_ref, bd_ref, kbd_ref, vbd_ref):
    xn = _rms(mem_ref[0], g_ref[...]).astype(BF16)
    kv = _dot(xn, w_ref[...])
    k = _head_rms(kv[:, :MEM_WIDTH], kg_ref[...], bd_ref)
    v = kv[:, MEM_WIDTH:]
    kt = k.T
    shape = (MEM_WIDTH, N_MEM)
    row_head = lax.shift_right_logical(lax.broadcasted_iota(jnp.int32, shape, 0), 6)
    col_head = lax.shift_right_logical(lax.broadcasted_iota(jnp.int32, (N_MEM, MEM_WIDTH), 1), 6)
    for h in range(MEM_HEADS):
        kbd_ref[0, :, h * N_MEM:(h + 1) * N_MEM] = jnp.where(row_head == h, kt, 0.0).astype(BF16)
        vbd_ref[0, h * N_MEM:(h + 1) * N_MEM, :] = jnp.where(col_head == h, v, 0.0).astype(BF16)


def _memory_kv(mem, g, w_kv, kg, bd):
    nb = mem.shape[0]
    const = lambda shape: pl.BlockSpec(shape, lambda b: (0,) * len(shape))
    return pl.pallas_call(
        _memkv_kernel,
        grid=(nb,),
        in_specs=[
            pl.BlockSpec((1, N_MEM, D_MODEL), lambda b: (b, 0, 0)),
            const((1, D_MODEL)),
            const((D_MODEL, 2 * MEM_WIDTH)),
            const((1, MEM_WIDTH)),
            const((MXU_DIM, MXU_DIM)),
        ],
        out_specs=[
            pl.BlockSpec((1, MEM_WIDTH, MEM_HEADS * N_MEM), lambda b: (b, 0, 0)),
            pl.BlockSpec((1, MEM_HEADS * N_MEM, MEM_WIDTH), lambda b: (b, 0, 0)),
        ],
        out_shape=[
            jax.ShapeDtypeStruct((nb, MEM_WIDTH, MEM_HEADS * N_MEM), BF16),
            jax.ShapeDtypeStruct((nb, MEM_HEADS * N_MEM, MEM_WIDTH), BF16),
        ],
        compiler_params=pltpu.CompilerParams(
            dimension_semantics=("arbitrary",), vmem_limit_bytes=VMEM_LIMIT_BYTES),
        name="memory_kv",
    )(mem, g, w_kv, kg, bd)


def _ffn_kernel(h_ref, g_ref, win_ref, wout_ref, o_ref):
    half = h_ref.shape[0] // 2
    rows = [slice(k * half, (k + 1) * half) for k in range(2)]
    gu = [_dot(_rms(h_ref[r, :], g_ref[...]).astype(BF16), win_ref[...]) for r in rows]
    for r, gu_k in zip(rows, gu):
        act = (jax.nn.silu(gu_k[:, :D_FF]) * gu_k[:, D_FF:]).astype(BF16)
        o_ref[r, :] = h_ref[r, :] + 0.5 * _dot(act, wout_ref[...])


def _ffn(h, g, w_in, w_out, tm):
    t = h.shape[0]
    resident = lambda shape: pl.BlockSpec(shape, lambda i: (0, 0), pipeline_mode=pl.Buffered(1))
    return pl.pallas_call(
        _ffn_kernel,
        grid=(t // tm,),
        in_specs=[
            pl.BlockSpec((tm, D_MODEL), lambda i: (i, 0)),
            resident((1, D_MODEL)),
            resident((D_MODEL, 2 * D_FF)),
            resident((D_FF, D_MODEL)),
        ],
        out_specs=pl.BlockSpec((tm, D_MODEL), lambda i: (i, 0)),
        out_shape=jax.ShapeDtypeStruct((t, D_MODEL), F32),
        compiler_params=pltpu.CompilerParams(
            dimension_semantics=("arbitrary",), vmem_limit_bytes=VMEM_LIMIT_BYTES),
        name="swiglu_ffn",
    )(h, g, w_in, w_out)


def _cross_probs(s):
    probs = []
    for h in range(MEM_HEADS):
        sh = s[:, h * N_MEM:(h + 1) * N_MEM]
        e = jnp.exp(sh - jnp.max(sh, axis=-1, keepdims=True))
        inv = 1.0 / jnp.sum(e, axis=-1, keepdims=True)
        probs.append((e * inv).astype(BF16))
    return jnp.concatenate(probs, axis=-1)


def _cross_attention(qm, kbd_ref, vbd_ref):
    return _dot(_cross_probs(_dot(qm, kbd_ref[0])), vbd_ref[0])


def _mix_out(h, tok, cross, wout_ref):
    cat = jnp.concatenate([tok.astype(BF16), cross.astype(BF16)], axis=-1)
    return h + _dot(cat, wout_ref[...])


def _lru_kernel(h_ref, g_ref, win_ref, cw_ref, cb_ref, wg_ref, brg_ref, big_ref, lam_ref,
                mqg_ref, bd_ref, kbd_ref, vbd_ref, wout_ref, o_ref,
                xs_ref, gq_ref, a_ref, b_ref, hprev_ref, *, ts):
    half = ts // LRU_PARTS

    @pl.when(pl.program_id(1) == 0)
    def _():
        xs_ref[0:SUBLANES, :] = jnp.zeros((SUBLANES, MIX_WIDTH), F32)
        hprev_ref[...] = jnp.zeros((SUBLANES, MIX_WIDTH), F32)

    def project(k):
        r0 = k * half
        hn = _rms(h_ref[0, r0:r0 + half, :], g_ref[...]).astype(BF16)
        proj = _dot(hn, win_ref[...])
        xs_ref[SUBLANES + r0:SUBLANES + r0 + half, :] = proj[:, :MIX_WIDTH]
        gq_ref[k] = proj[:, MIX_WIDTH:]

    def gates(k):
        r0 = k * half
        xe = xs_ref[r0:r0 + half + SUBLANES, :]
        acc = xe * cw_ref[0:1, :]
        for tap in range(1, CONV_WIDTH):
            acc = pltpu.roll(acc, 1, 0) + xe * cw_ref[tap:tap + 1, :]
        xc = cb_ref[...] + acc[SUBLANES:, :]
        xcb = xc.astype(BF16)
        r_parts, i_parts = [], []
        for c in range(MIX_WIDTH // MXU_DIM):
            gg = _dot(xcb[:, c * MXU_DIM:(c + 1) * MXU_DIM], wg_ref[c])
            r_parts.append(gg[:, :MXU_DIM])
            i_parts.append(gg[:, MXU_DIM:])
        r = jax.nn.sigmoid(jnp.concatenate(r_parts, axis=-1) + brg_ref[...])
        gi = jax.nn.sigmoid(jnp.concatenate(i_parts, axis=-1) + big_ref[...])
        log_a = -LRU_C * r * _softplus(-lam_ref[...])
        t = jnp.tanh(-log_a)
        a = jnp.exp(log_a)
        t2 = 2.0 * t
        root = jnp.where(t > 0.0, t2 * lax.rsqrt(t2 * (1.0 + t)), 0.0)
        bx = root * (gi * xc)
        for lt in range(MIX_WIDTH // LANES):
            a_ref[lt, r0:r0 + half, :] = a[:, lt * LANES:(lt + 1) * LANES]
            b_ref[lt, r0:r0 + half, :] = bx[:, lt * LANES:(lt + 1) * LANES]

    row = lax.broadcasted_iota(jnp.int32, (SUBLANES, LANES), 0)

    def scan(k, hprev):
        blk = SUBLANES * SUBLANES
        hprev = list(hprev)
        for i in range(half // blk):
            base = k * half + i * blk
            for lt in range(MIX_WIDTH // LANES):
                a = [a_ref[lt, pl.ds(base + j, SUBLANES, stride=SUBLANES), :] for j in range(SUBLANES)]
                b = [b_ref[lt, pl.ds(base + j, SUBLANES, stride=SUBLANES), :] for j in range(SUBLANES)]
                hloc, prod = [b[0]], [a[0]]
                for j in range(1, SUBLANES):
                    hloc.append(a[j] * hloc[-1] + b[j])
                    prod.append(a[j] * prod[-1])
                ea, eb = prod[-1], hloc[-1]
                for sh in (1, 2, 4):
                    keep = row >= sh
                    a_sh = jnp.where(keep, pltpu.roll(ea, sh, 0), 1.0)
                    b_sh = jnp.where(keep, pltpu.roll(eb, sh, 0), 0.0)
                    eb = eb + ea * b_sh
                    ea = ea * a_sh
                ends = eb + ea * hprev[lt]
                incoming = jnp.where(row >= 1, pltpu.roll(ends, 1, 0), hprev[lt])
                for j in range(SUBLANES):
                    b_ref[lt, pl.ds(base + j, SUBLANES, stride=SUBLANES), :] = hloc[j] + prod[j] * incoming
                hprev[lt] = jnp.broadcast_to(ends[SUBLANES - 1:SUBLANES, :], (SUBLANES, LANES))
        return hprev

    def finish(k):
        r0 = k * half
        gq = gq_ref[k]
        hs = jnp.concatenate([b_ref[lt, r0:r0 + half, :] for lt in range(MIX_WIDTH // LANES)], axis=-1)
        tok = hs * jax.nn.gelu(gq[:, :MIX_WIDTH])
        qm = (_head_rms(gq[:, MIX_WIDTH:], mqg_ref[...], bd_ref) * SCORE_SCALE).astype(BF16)
        cross = _cross_attention(qm, kbd_ref, vbd_ref)
        o_ref[0, r0:r0 + half, :] = _mix_out(h_ref[0, r0:r0 + half, :], tok, cross, wout_ref)

    for k in range(LRU_PARTS):
        project(k)
        gates(k)
    xs_ref[0:SUBLANES, :] = xs_ref[ts:ts + SUBLANES, :]
    state = [hprev_ref[:, lt * LANES:(lt + 1) * LANES] for lt in range(MIX_WIDTH // LANES)]
    for k in range(LRU_PARTS):
        state = scan(k, state)
        finish(k)
    hprev_ref[...] = jnp.concatenate(state, axis=-1)


def _lru_mix(h, g, w_in, conv_w, conv_b, wg, b_rg, b_ig, lam, mqg, bd, kbd, vbd, w_out, ts):
    nb, s, _ = h.shape
    const = lambda shape: pl.BlockSpec(shape, lambda b, i: (0,) * len(shape), pipeline_mode=pl.Buffered(1))
    per_batch = lambda shape: pl.BlockSpec((1,) + shape, lambda b, i: (b, 0, 0))
    return pl.pallas_call(
        functools.partial(_lru_kernel, ts=ts),
        grid=(nb, s // ts),
        in_specs=[
            pl.BlockSpec((1, ts, D_MODEL), lambda b, i: (b, i, 0)),
            const((1, D_MODEL)),
            const((D_MODEL, 2 * MIX_WIDTH + MEM_WIDTH)),
            const((CONV_WIDTH, MIX_WIDTH)),
            const((1, MIX_WIDTH)),
            const((MIX_WIDTH // MXU_DIM, MXU_DIM, 2 * MXU_DIM)),
            const((1, MIX_WIDTH)),
            const((1, MIX_WIDTH)),
            const((1, MIX_WIDTH)),
            const((1, MEM_WIDTH)),
            const((MXU_DIM, MXU_DIM)),
            per_batch((MEM_WIDTH, MEM_HEADS * N_MEM)),
            per_batch((MEM_HEADS * N_MEM, MEM_WIDTH)),
            const((D_MODEL, D_MODEL)),
        ],
        out_specs=pl.BlockSpec((1, ts, D_MODEL), lambda b, i: (b, i, 0)),
        out_shape=jax.ShapeDtypeStruct(h.shape, F32),
        scratch_shapes=[
            pltpu.VMEM((ts + SUBLANES, MIX_WIDTH), F32),
            pltpu.VMEM((LRU_PARTS, ts // LRU_PARTS, MIX_WIDTH + MEM_WIDTH), F32),
            pltpu.VMEM((MIX_WIDTH // LANES, ts, LANES), F32),
            pltpu.VMEM((MIX_WIDTH // LANES, ts, LANES), F32),
            pltpu.VMEM((SUBLANES, MIX_WIDTH), F32),
        ],
        compiler_params=pltpu.CompilerParams(
            dimension_semantics=("arbitrary", "arbitrary"), vmem_limit_bytes=VMEM_LIMIT_BYTES),
        name="rglru_mix",
    )(h, g, w_in, conv_w, conv_b, wg, b_rg, b_ig, lam, mqg, bd, kbd, vbd, w_out)


def _split3(x):
    p0 = x.astype(BF16)
    r1 = x - p0.astype(F32)
    p1 = r1.astype(BF16)
    p2 = (r1 - p1.astype(F32)).astype(BF16)
    return p0, p1, p2


def _fox_proj_kernel(h_ref, g_ref, w_ref, qg_ref, kg_ref, mqg_ref, bf_ref, bd_ref, tri_ref,
                     q_ref, k_ref, vt_ref, qm_ref, e_ref, base_ref, rel_ref, abs_ref, *, tiles_per_attn):
    t = pl.program_id(1)

    @pl.when(t == 0)
    def _():
        abs_ref[...] = jnp.zeros((1, F_PAD), F32)

    @pl.when(t % tiles_per_attn == 0)
    def _():
        rel_ref[...] = jnp.zeros((1, F_PAD), F32)

    hn = _rms(h_ref[0], g_ref[...]).astype(BF16)
    proj = _dot(hn, w_ref[...])
    w = MIX_WIDTH
    q_ref[0] = (_head_rms(proj[:, :w], qg_ref[...], bd_ref) * (SCORE_SCALE * LOG2E)).astype(BF16)
    k_ref[0] = _head_rms(proj[:, w:2 * w], kg_ref[...], bd_ref).astype(BF16)
    vt = proj[:, 2 * w:3 * w].T.astype(BF16)
    pad_rows = VT_ROWS - HEAD_DIM
    ones_row = jnp.where(lax.broadcasted_iota(jnp.int32, (pad_rows, vt.shape[1]), 0) == 0, 1.0, 0.0)
    for hd in range(MIX_HEADS):
        vt_ref[0, hd, :HEAD_DIM, :] = vt[hd * HEAD_DIM:(hd + 1) * HEAD_DIM, :]
        vt_ref[0, hd, HEAD_DIM:, :] = ones_row.astype(BF16)
    qm = proj[:, 3 * w:3 * w + MEM_WIDTH]
    qm_ref[0] = (_head_rms(qm, mqg_ref[...], bd_ref) * SCORE_SCALE).astype(BF16)

    log_f = -_softplus(-(proj[:, 3 * w + MEM_WIDTH:] + bf_ref[...]))
    tri = tri_ref[...]
    rel = rel_ref[...] + sum(_dot(tri, p) for p in _split3(log_f))
    last = rel[rel.shape[0] - 1:, :]
    lane = lax.broadcasted_iota(jnp.int32, rel.shape, 1)
    e0, e1, e2 = _split3(jnp.where(lane < MIX_HEADS, -LOG2E * rel, 0.0))
    e_ref[0] = (e0.astype(F32) + pltpu.roll(e1.astype(F32), E_STRIDE, 1)
                + pltpu.roll(e2.astype(F32), 2 * E_STRIDE, 1)).astype(BF16)
    base_ref[0, 0] = LOG2E * abs_ref[...]
    rel_ref[...] = last

    @pl.when(t % tiles_per_attn == tiles_per_attn - 1)
    def _():
        abs_ref[...] = abs_ref[...] + last


def _fox_proj(h, g, w, qg, kg, mqg, bf, bd, tri, ts, tq):
    nb, s, _ = h.shape
    const = lambda shape: pl.BlockSpec(shape, lambda b, i: (0,) * len(shape), pipeline_mode=pl.Buffered(1))
    tile = lambda width: pl.BlockSpec((1, ts, width), lambda b, i: (b, i, 0))
    return pl.pallas_call(
        functools.partial(_fox_proj_kernel, tiles_per_attn=tq // ts),
        grid=(nb, s // ts),
        in_specs=[
            tile(D_MODEL),
            const((1, D_MODEL)),
            const((D_MODEL, FOX_COLS)),
            const((1, MIX_WIDTH)),
            const((1, MIX_WIDTH)),
            const((1, MEM_WIDTH)),
            const((1, F_PAD)),
            const((MXU_DIM, MXU_DIM)),
            const((ts, ts)),
        ],
        out_specs=[
            tile(MIX_WIDTH), tile(MIX_WIDTH),
            pl.BlockSpec((1, MIX_HEADS, VT_ROWS, ts), lambda b, i: (b, 0, 0, i)),
            tile(MEM_WIDTH), tile(F_PAD),
            pl.BlockSpec((1, 1, 1, F_PAD), lambda b, i: (b, i, 0, 0)),
        ],
        out_shape=[
            jax.ShapeDtypeStruct((nb, s, MIX_WIDTH), BF16),
            jax.ShapeDtypeStruct((nb, s, MIX_WIDTH), BF16),
            jax.ShapeDtypeStruct((nb, MIX_HEADS, VT_ROWS, s), BF16),
            jax.ShapeDtypeStruct((nb, s, MEM_WIDTH), BF16),
            jax.ShapeDtypeStruct((nb, s, F_PAD), BF16),
            jax.ShapeDtypeStruct((nb, s // ts, 1, F_PAD), F32),
        ],
        scratch_shapes=[pltpu.VMEM((1, F_PAD), F32), pltpu.VMEM((1, F_PAD), F32)],
        compiler_params=pltpu.CompilerParams(
            dimension_semantics=("arbitrary", "arbitrary"), vmem_limit_bytes=VMEM_LIMIT_BYTES),
        name="fox_proj",
    )(h, g, w, qg, kg, mqg, bf, bd, tri)


def _fox_attn_kernel(base_ref, q_ref, k_ref, e_ref, vt_ref, o_ref, qc_ref, m_ref, acc_ref,
                     s_ref, mc_ref, p_ref, al_ref, *, tq, n_tiles):
    b = pl.program_id(0)
    hp = pl.program_id(1)
    lane = lax.broadcasted_iota(jnp.int32, (tq, LANES), 1)

    def values(a, j):
        k_start = pl.multiple_of(j * tq, tq)
        acc_ref[a] = al_ref[a] * acc_ref[a] + _dot(vt_ref[0, a, :, pl.ds(k_start, tq)], p_ref[a])

    def finish_tile(qi):
        last = jnp.where(qi == 0, qi, qi - 1)
        values(0, last)
        values(1, last)
        out_t = jnp.concatenate(
            [acc_ref[a, :HEAD_DIM, :] * (1.0 / acc_ref[a, HEAD_DIM:HEAD_DIM + 1, :]) for a in range(2)],
            axis=0)
        o_ref[0, pl.ds(pl.multiple_of(qi * tq, tq), tq), :] = out_t.T.astype(BF16)

    acc_ref[...] = jnp.zeros(acc_ref.shape, F32)
    al_ref[...] = jnp.zeros(al_ref.shape, F32)
    p_ref[...] = jnp.zeros(p_ref.shape, BF16)

    def query_tile(qi, carry):
        finish_tile(jnp.maximum(qi - 1, 0))
        q_start = pl.multiple_of(qi * tq, tq)
        qf = q_ref[0, pl.ds(q_start, tq), :].astype(F32)
        for a in range(2):
            h = 2 * hp + a
            own = (lane < HEAD_DIM) if a == 0 else (lane >= HEAD_DIM)
            sel = (jnp.bitwise_and(lane, E_STRIDE - 1) == h) & (lane < 3 * E_STRIDE)
            qc_ref[a] = jnp.concatenate(
                [jnp.where(own, qf, 0.0), jnp.where(sel, 1.0, 0.0)], axis=-1).astype(BF16)
        m_ref[...] = jnp.full(m_ref.shape, -jnp.inf, F32)
        acc_ref[...] = jnp.zeros(acc_ref.shape, F32)

        def scores(a, j, slot, masked):
            k_start = pl.multiple_of(j * tq, tq)
            kc = jnp.concatenate(
                [k_ref[0, pl.ds(k_start, tq), :], e_ref[0, pl.ds(k_start, tq), :]], axis=-1)
            raw = _dot_nt(kc, qc_ref[a])
            if masked:
                key = lax.broadcasted_iota(jnp.int32, (tq, tq), 0)
                qry = lax.broadcasted_iota(jnp.int32, (tq, tq), 1)
                raw = jnp.where(key <= qry, raw, -jnp.inf)
            s_ref[a, slot] = raw
            mc_ref[a, slot] = jnp.max(raw, axis=0, keepdims=True)

        def softmax(a, j, slot):
            h = 2 * hp + a
            delta = (base_ref[(b * n_tiles + qi) * E_STRIDE + h]
                     - base_ref[(b * n_tiles + j) * E_STRIDE + h])
            m_prev = m_ref[a]
            m_new = jnp.maximum(m_prev, mc_ref[a, slot] + delta)
            al_ref[a] = jnp.exp2(m_prev - m_new)
            p_ref[a] = jnp.exp2(s_ref[a, slot] - (m_new - delta)).astype(BF16)
            m_ref[a] = m_new

        def round_(t, slot, issue_scores):
            prev = jnp.where(t == 1, qi, t - 2)
            values(0, prev)
            values(1, prev)
            if issue_scores:
                scores(0, t, 1 - slot, False)
                scores(1, t, 1 - slot, False)
            softmax(0, t - 1, slot)
            softmax(1, t - 1, slot)

        scores(0, qi, 0, True)
        scores(1, qi, 0, True)
        scores(0, 0, 1, False)
        scores(1, 0, 1, False)
        softmax(0, qi, 0)
        softmax(1, qi, 0)

        def round_pair(i, c):
            round_(2 * i + 1, 1, True)
            round_(2 * i + 2, 0, True)
            return c

        lax.fori_loop(0, (qi - 1) // 2, round_pair, 0)

        @pl.when(qi % 2 == 1)
        def _():
            round_(qi, 1, False)

        @pl.when((qi % 2 == 0) & (qi > 0))
        def _():
            round_(qi - 1, 1, True)
            round_(qi, 0, False)

        return carry

    lax.fori_loop(0, n_tiles, query_tile, 0)
    finish_tile(n_tiles - 1)


def _fox_attention(base, q, k, e, vt, tq):
    nb, s, _ = q.shape
    seq_block = lambda width: pl.BlockSpec((1, s, width), lambda b, p: (b, 0, p))
    return pl.pallas_call(
        functools.partial(_fox_attn_kernel, tq=tq, n_tiles=s // tq),
        grid=(nb, MIX_WIDTH // LANES),
        in_specs=[
            pl.BlockSpec(memory_space=pltpu.SMEM),
            seq_block(LANES),
            seq_block(LANES),
            pl.BlockSpec((1, s, F_PAD), lambda b, p: (b, 0, 0)),
            pl.BlockSpec((1, 2, VT_ROWS, s), lambda b, p: (b, p, 0, 0)),
        ],
        out_specs=seq_block(LANES),
        out_shape=jax.ShapeDtypeStruct((nb, s, MIX_WIDTH), BF16),
        scratch_shapes=[
            pltpu.VMEM((2, tq, 2 * LANES), BF16),
            pltpu.VMEM((2, 1, tq), F32),
            pltpu.VMEM((2, VT_ROWS, tq), F32),
            pltpu.VMEM((2, 2, tq, tq), F32),
            pltpu.VMEM((2, 2, 1, tq), F32),
            pltpu.VMEM((2, tq, tq), BF16),
            pltpu.VMEM((2, 1, tq), F32),
        ],
        compiler_params=pltpu.CompilerParams(
            dimension_semantics=("arbitrary", "arbitrary"), vmem_limit_bytes=VMEM_LIMIT_BYTES),
        name="fox_attention",
    )(base, q, k, e, vt)


def _fox_out_kernel(h_ref, tok_ref, qm_ref, kbd_ref, vbd_ref, wout_ref, o_ref):
    half = h_ref.shape[1] // 2
    rows = [slice(k * half, (k + 1) * half) for k in range(2)]
    s = [_dot(qm_ref[0, r, :], kbd_ref[0]) for r in rows]
    cross = [None, None]
    cross[0] = _dot(_cross_probs(s[0]), vbd_ref[0])
    p1 = _cross_probs(s[1])
    o_ref[0, rows[0], :] = _mix_out(h_ref[0, rows[0], :], tok_ref[0, rows[0], :], cross[0], wout_ref)
    cross[1] = _dot(p1, vbd_ref[0])
    o_ref[0, rows[1], :] = _mix_out(h_ref[0, rows[1], :], tok_ref[0, rows[1], :], cross[1], wout_ref)


def _fox_out(h, tok, qm, kbd, vbd, w_out, ts):
    nb, s, _ = h.shape
    tile = lambda width: pl.BlockSpec((1, ts, width), lambda b, i: (b, i, 0))
    per_batch = lambda shape: pl.BlockSpec((1,) + shape, lambda b, i: (b, 0, 0))
    return pl.pallas_call(
        _fox_out_kernel,
        grid=(nb, s // ts),
        in_specs=[
            tile(D_MODEL), tile(MIX_WIDTH), tile(MEM_WIDTH),
            per_batch((MEM_WIDTH, MEM_HEADS * N_MEM)),
            per_batch((MEM_HEADS * N_MEM, MEM_WIDTH)),
            pl.BlockSpec((D_MODEL, D_MODEL), lambda b, i: (0, 0), pipeline_mode=pl.Buffered(1)),
        ],
        out_specs=tile(D_MODEL),
        out_shape=jax.ShapeDtypeStruct(h.shape, F32),
        compiler_params=pltpu.CompilerParams(
            dimension_semantics=("arbitrary", "arbitrary"), vmem_limit_bytes=VMEM_LIMIT_BYTES),
        name="fox_out",
    )(h, tok, qm, kbd, vbd, w_out)


def _row(v):
    return v.reshape(1, -1).astype(F32)


def _tile_heads(g, heads):
    return jnp.tile(g.astype(F32), heads).reshape(1, -1)


def _gate_weights(w_rg, w_ig):
    def block_diag(w):
        groups = w.reshape(MIX_HEADS // HEADS_PER_MXU_TILE, HEADS_PER_MXU_TILE, HEAD_DIM, HEAD_DIM)
        eye = jnp.eye(HEADS_PER_MXU_TILE, dtype=w.dtype)
        bd = jnp.einsum('ghij,hk->ghikj', groups, eye)
        return bd.reshape(-1, MXU_DIM, MXU_DIM)
    return jnp.concatenate([block_diag(w_rg), block_diag(w_ig)], axis=-1).astype(BF16)


def _fox_weights(w):
    qkv = w[:, :3 * MIX_WIDTH]
    f = w[:, 3 * MIX_WIDTH:3 * MIX_WIDTH + MIX_HEADS]
    qm = w[:, 3 * MIX_WIDTH + MIX_HEADS:]
    f = jnp.pad(f, ((0, 0), (0, F_PAD - MIX_HEADS)))
    return jnp.concatenate([qkv, qm, f], axis=-1).astype(BF16)


def kernel(x, mem, mem_norm_g, mem_w_kv, mem_k_norm_g, ffn1_norm_g, ffn1_w_in, ffn1_w_out, mix_norm_g, mix_w_out, memq_norm_g, ffn2_norm_g, ffn2_w_in, ffn2_w_out, lru_w_in, lru_conv_w, lru_conv_b, lru_w_rg, lru_b_rg, lru_w_ig, lru_b_ig, lru_lambda, fox_w_in, fox_b_f, fox_q_norm_g, fox_k_norm_g):
    nb, s, d = x.shape
    assert d == D_MODEL and mem.shape[1:] == (N_MEM, D_MODEL)
    ts = min(ROW_TILE, s)
    tq = min(ATTN_TILE, s)
    assert s % ts == 0 and s % tq == 0 and (nb * s) % ts == 0

    head_id = jnp.arange(MXU_DIM) // HEAD_DIM
    bd = (head_id[:, None] == head_id[None, :]).astype(BF16)
    tri = (jnp.arange(ts)[:, None] >= jnp.arange(ts)[None, :]).astype(BF16)

    kbd, vbd = _memory_kv(mem, _row(mem_norm_g), mem_w_kv.astype(BF16),
                          _tile_heads(mem_k_norm_g, MEM_HEADS), bd)

    def ffn(h, g, w_in, w_out):
        out = _ffn(h.reshape(nb * s, d), _row(g), w_in.astype(BF16), w_out.astype(BF16),
                   min(FFN_ROW_TILE, nb * s))
        return out.reshape(nb, s, d)

    h = x
    h = ffn(h, ffn1_norm_g[0], ffn1_w_in[0], ffn1_w_out[0])
    h = _lru_mix(h, _row(mix_norm_g[0]), lru_w_in[0].astype(BF16), lru_conv_w[0].astype(F32),
                 _row(lru_conv_b[0]), _gate_weights(lru_w_rg[0], lru_w_ig[0]), _row(lru_b_rg[0]),
                 _row(lru_b_ig[0]), _row(lru_lambda[0]), _tile_heads(memq_norm_g[0], MEM_HEADS),
                 bd, kbd, vbd, mix_w_out[0].astype(BF16), min(LRU_ROW_TILE, s))
    h = ffn(h, ffn2_norm_g[0], ffn2_w_in[0], ffn2_w_out[0])
    h = ffn(h, ffn1_norm_g[1], ffn1_w_in[1], ffn1_w_out[1])
    bf = jnp.pad(fox_b_f[0].astype(F32), (0, F_PAD - MIX_HEADS)).reshape(1, F_PAD)
    q, k, vt, qm, e, base = _fox_proj(h, _row(mix_norm_g[1]), _fox_weights(fox_w_in[0]),
                                      _tile_heads(fox_q_norm_g[0], MIX_HEADS),
                                      _tile_heads(fox_k_norm_g[0], MIX_HEADS),
                                      _tile_heads(memq_norm_g[1], MEM_HEADS), bf, bd, tri, ts, tq)
    base = base[:, ::tq // ts, 0, :E_STRIDE].reshape(-1)
    tok = _fox_attention(base, q, k, e, vt, tq)
    h = _fox_out(h, tok, qm, kbd, vbd, mix_w_out[1].astype(BF16), min(LRU_ROW_TILE, s))
    h = ffn(h, ffn2_norm_g[1], ffn2_w_in[1], ffn2_w_out[1])
    return h
```

```python
import functools

import jax
import jax.numpy as jnp
from jax import lax
from jax.experimental import pallas as pl
from jax.experimental.pallas import tpu as pltpu

F32 = jnp.float32
BF16 = jnp.bfloat16

D_MODEL = 1024
HEAD_DIM = 64
MIX_HEADS = 12
MIX_WIDTH = MIX_HEADS * HEAD_DIM
MEM_HEADS = 4
MEM_WIDTH = MEM_HEADS * HEAD_DIM
N_MEM = 256
D_FF = 2816
CONV_WIDTH = 4
LRU_C = 8.0
NORM_EPS = 1e-6
SCORE_SCALE = HEAD_DIM ** -0.5
LOG2E = 1.4426950408889634
VT_ROWS = 80

LANES = 128
SUBLANES = 8
MXU_DIM = 256
HEADS_PER_MXU_TILE = MXU_DIM // HEAD_DIM
F_PAD = LANES
FOX_COLS = 3 * MIX_WIDTH + MEM_WIDTH + F_PAD
E_STRIDE = 16
VMEM_LIMIT_BYTES = 56 * 1024 * 1024

ROW_TILE = 256
FFN_ROW_TILE = 512
MIX_ROW_TILE = 512
ATTN_TILE = 512


def _dot(a, b):
    return jnp.dot(a, b, preferred_element_type=F32)


def _dot_nt(a, b):
    return lax.dot_general(a, b, (((1,), (1,)), ((), ())), preferred_element_type=F32)


def _rms(x, g):
    ms = jnp.mean(x * x, axis=-1, keepdims=True)
    return x * lax.rsqrt(ms + NORM_EPS) * g


def _head_rms(x, g, bd_ref):
    outs = []
    for c in range(x.shape[1] // MXU_DIM):
        xc = x[:, c * MXU_DIM:(c + 1) * MXU_DIM]
        x2 = xc * xc
        hi = x2.astype(BF16)
        lo = (x2 - hi.astype(F32)).astype(BF16)
        ms = (_dot(hi, bd_ref[...]) + _dot(lo, bd_ref[...])) * (1.0 / HEAD_DIM)
        outs.append(xc * lax.rsqrt(ms + NORM_EPS))
    y = outs[0] if len(outs) == 1 else jnp.concatenate(outs, axis=-1)
    return y * g


def _softplus(x):
    return jnp.maximum(x, 0.0) + jnp.log1p(jnp.exp(-jnp.abs(x)))


def _memkv_kernel(mem_ref, g_ref, w_ref, kg_ref, bd_ref, kbd_ref, vbd_ref):
    xn = _rms(mem_ref[0], g_ref[...]).astype(BF16)
    kv = _dot(xn, w_ref[...])
    k = _head_rms(kv[:, :MEM_WIDTH], kg_ref[...], bd_ref)
    v = kv[:, MEM_WIDTH:]
    kt = k.T
    shape = (MEM_WIDTH, N_MEM)
    row_head = lax.shift_right_logical(lax.broadcasted_iota(jnp.int32, shape, 0), 6)
    col_head = lax.shift_right_logical(lax.broadcasted_iota(jnp.int32, (N_MEM, MEM_WIDTH), 1), 6)
    for h in range(MEM_HEADS):
        kbd_ref[0, :, h * N_MEM:(h + 1) * N_MEM] = jnp.where(row_head == h, kt, 0.0).astype(BF16)
        vbd_ref[0, h * N_MEM:(h + 1) * N_MEM, :] = jnp.where(col_head == h, v, 0.0).astype(BF16)


def _memory_kv(mem, g, w_kv, kg, bd):
    nb = mem.shape[0]
    const = lambda shape: pl.BlockSpec(shape, lambda b: (0,) * len(shape))
    return pl.pallas_call(
        _memkv_kernel,
        grid=(nb,),
        in_specs=[
            pl.BlockSpec((1, N_MEM, D_MODEL), lambda b: (b, 0, 0)),
            const((1, D_MODEL)),
            const((D_MODEL, 2 * MEM_WIDTH)),
            const((1, MEM_WIDTH)),
            const((MXU_DIM, MXU_DIM)),
        ],
        out_specs=[
            pl.BlockSpec((1, MEM_WIDTH, MEM_HEADS * N_MEM), lambda b: (b, 0, 0)),
            pl.BlockSpec((1, MEM_HEADS * N_MEM, MEM_WIDTH), lambda b: (b, 0, 0)),
        ],
        out_shape=[
            jax.ShapeDtypeStruct((nb, MEM_WIDTH, MEM_HEADS * N_MEM), BF16),
            jax.ShapeDtypeStruct((nb, MEM_HEADS * N_MEM, MEM_WIDTH), BF16),
        ],
        compiler_params=pltpu.CompilerParams(
            dimension_semantics=("arbitrary",), vmem_limit_bytes=VMEM_LIMIT_BYTES),
        name="memory_kv",
    )(mem, g, w_kv, kg, bd)


def _ffn_kernel(h_ref, g_ref, win_ref, wout_ref, o_ref):
    half = h_ref.shape[0] // 2
    rows = [slice(k * half, (k + 1) * half) for k in range(2)]
    gu = [_dot(_rms(h_ref[r, :], g_ref[...]).astype(BF16), win_ref[0]) for r in rows]
    for r, gu_k in zip(rows, gu):
        act = (jax.nn.silu(gu_k[:, :D_FF]) * gu_k[:, D_FF:]).astype(BF16)
        o_ref[r, :] = h_ref[r, :] + 0.5 * _dot(act, wout_ref[0])


def _ffn(h, g, w_in, w_out, layer, tm):
    t = h.shape[0]
    resident = lambda shape, first=0: pl.BlockSpec(
        shape, lambda i: (first,) + (0,) * (len(shape) - 1), pipeline_mode=pl.Buffered(1))
    return pl.pallas_call(
        _ffn_kernel,
        grid=(t // tm,),
        in_specs=[
            pl.BlockSpec((tm, D_MODEL), lambda i: (i, 0)),
            resident((1, D_MODEL)),
            resident((1, D_MODEL, 2 * D_FF), layer),
            resident((1, D_FF, D_MODEL), layer),
        ],
        out_specs=pl.BlockSpec((tm, D_MODEL), lambda i: (i, 0)),
        out_shape=jax.ShapeDtypeStruct((t, D_MODEL), F32),
        compiler_params=pltpu.CompilerParams(
            dimension_semantics=("arbitrary",), vmem_limit_bytes=VMEM_LIMIT_BYTES),
        name="swiglu_ffn",
    )(h, g, w_in, w_out)


def _cross_probs(s):
    probs = []
    for h in range(MEM_HEADS):
        sh = s[:, h * N_MEM:(h + 1) * N_MEM]
        e = jnp.exp(sh - jnp.max(sh, axis=-1, keepdims=True))
        inv = 1.0 / jnp.sum(e, axis=-1, keepdims=True)
        probs.append((e * inv).astype(BF16))
    return jnp.concatenate(probs, axis=-1)


def _cross_attention(qm, kbd_ref, vbd_ref):
    return _dot(_cross_probs(_dot(qm, kbd_ref[0])), vbd_ref[0])


def _mix_out(h, tok, cross, wout_ref):
    cat = jnp.concatenate([tok.astype(BF16), cross.astype(BF16)], axis=-1)
    return h + _dot(cat, wout_ref[...])


def _lru_kernel(h_ref, g_ref, win_ref, cw_ref, cb_ref, wg_ref, brg_ref, big_ref, lam_ref,
                mqg_ref, bd_ref, kbd_ref, vbd_ref, wout_ref, o_ref,
                xs_ref, gq_ref, a_ref, b_ref, hprev_ref, *, ts):
    half = ts // 2

    @pl.when(pl.program_id(1) == 0)
    def _():
        xs_ref[0:SUBLANES, :] = jnp.zeros((SUBLANES, MIX_WIDTH), F32)
        hprev_ref[...] = jnp.zeros((SUBLANES, MIX_WIDTH), F32)

    def project(k):
        r0 = k * half
        hn = _rms(h_ref[0, r0:r0 + half, :], g_ref[...]).astype(BF16)
        proj = _dot(hn, win_ref[...])
        xs_ref[SUBLANES + r0:SUBLANES + r0 + half, :] = proj[:, :MIX_WIDTH]
        gq_ref[k] = proj[:, MIX_WIDTH:]

    def gates(k):
        r0 = k * half
        xe = xs_ref[r0:r0 + half + SUBLANES, :]
        acc = xe * cw_ref[0:1, :]
        for tap in range(1, CONV_WIDTH):
            acc = pltpu.roll(acc, 1, 0) + xe * cw_ref[tap:tap + 1, :]
        xc = cb_ref[...] + acc[SUBLANES:, :]
        xcb = xc.astype(BF16)
        r_parts, i_parts = [], []
        for c in range(MIX_WIDTH // MXU_DIM):
            gg = _dot(xcb[:, c * MXU_DIM:(c + 1) * MXU_DIM], wg_ref[c])
            r_parts.append(gg[:, :MXU_DIM])
            i_parts.append(gg[:, MXU_DIM:])
        r = jax.nn.sigmoid(jnp.concatenate(r_parts, axis=-1) + brg_ref[...])
        gi = jax.nn.sigmoid(jnp.concatenate(i_parts, axis=-1) + big_ref[...])
        log_a = -LRU_C * r * _softplus(-lam_ref[...])
        t = jnp.tanh(-log_a)
        a = jnp.exp(log_a)
        t2 = 2.0 * t
        root = jnp.where(t > 0.0, t2 * lax.rsqrt(t2 * (1.0 + t)), 0.0)
        bx = root * (gi * xc)
        for lt in range(MIX_WIDTH // LANES):
            a_ref[lt, r0:r0 + half, :] = a[:, lt * LANES:(lt + 1) * LANES]
            b_ref[lt, r0:r0 + half, :] = bx[:, lt * LANES:(lt + 1) * LANES]

    row = lax.broadcasted_iota(jnp.int32, (SUBLANES, LANES), 0)

    def scan(k, hprev):
        blk = SUBLANES * SUBLANES
        hprev = list(hprev)
        for i in range(half // blk):
            base = k * half + i * blk
            for lt in range(MIX_WIDTH // LANES):
                a = [a_ref[lt, pl.ds(base + j, SUBLANES, stride=SUBLANES), :] for j in range(SUBLANES)]
                b = [b_ref[lt, pl.ds(base + j, SUBLANES, stride=SUBLANES), :] for j in range(SUBLANES)]
                hloc, prod = [b[0]], [a[0]]
                for j in range(1, SUBLANES):
                    hloc.append(a[j] * hloc[-1] + b[j])
                    prod.append(a[j] * prod[-1])
                ea, eb = prod[-1], hloc[-1]
                for sh in (1, 2, 4):
                    keep = row >= sh
                    a_sh = jnp.where(keep, pltpu.roll(ea, sh, 0), 1.0)
                    b_sh = jnp.where(keep, pltpu.roll(eb, sh, 0), 0.0)
                    eb = eb + ea * b_sh
                    ea = ea * a_sh
                ends = eb + ea * hprev[lt]
                incoming = jnp.where(row >= 1, pltpu.roll(ends, 1, 0), hprev[lt])
                for j in range(SUBLANES):
                    b_ref[lt, pl.ds(base + j, SUBLANES, stride=SUBLANES), :] = hloc[j] + prod[j] * incoming
                hprev[lt] = jnp.broadcast_to(ends[SUBLANES - 1:SUBLANES, :], (SUBLANES, LANES))
        return hprev

    def finish(k):
        r0 = k * half
        gq = gq_ref[k]
        hs = jnp.concatenate([b_ref[lt, r0:r0 + half, :] for lt in range(MIX_WIDTH // LANES)], axis=-1)
        tok = hs * jax.nn.gelu(gq[:, :MIX_WIDTH])
        qm = (_head_rms(gq[:, MIX_WIDTH:], mqg_ref[...], bd_ref) * SCORE_SCALE).astype(BF16)
        cross = _cross_attention(qm, kbd_ref, vbd_ref)
        o_ref[0, r0:r0 + half, :] = _mix_out(h_ref[0, r0:r0 + half, :], tok, cross, wout_ref)

    for k in range(2):
        project(k)
        gates(k)
    xs_ref[0:SUBLANES, :] = xs_ref[ts:ts + SUBLANES, :]
    state = [hprev_ref[:, lt * LANES:(lt + 1) * LANES] for lt in range(MIX_WIDTH // LANES)]
    for k in range(2):
        state = scan(k, state)
        finish(k)
    hprev_ref[...] = jnp.concatenate(state, axis=-1)


def _lru_mix(h, g, w_in, conv_w, conv_b, wg, b_rg, b_ig, lam, mqg, bd, kbd, vbd, w_out, ts):
    nb, s, _ = h.shape
    const = lambda shape: pl.BlockSpec(shape, lambda b, i: (0,) * len(shape), pipeline_mode=pl.Buffered(1))
    per_batch = lambda shape: pl.BlockSpec((1,) + shape, lambda b, i: (b, 0, 0))
    return pl.pallas_call(
        functools.partial(_lru_kernel, ts=ts),
        grid=(nb, s // ts),
        in_specs=[
            pl.BlockSpec((1, ts, D_MODEL), lambda b, i: (b, i, 0)),
            const((1, D_MODEL)),
            const((D_MODEL, 2 * MIX_WIDTH + MEM_WIDTH)),
            const((CONV_WIDTH, MIX_WIDTH)),
            const((1, MIX_WIDTH)),
            const((MIX_WIDTH // MXU_DIM, MXU_DIM, 2 * MXU_DIM)),
            const((1, MIX_WIDTH)),
            const((1, MIX_WIDTH)),
            const((1, MIX_WIDTH)),
            const((1, MEM_WIDTH)),
            const((MXU_DIM, MXU_DIM)),
            per_batch((MEM_WIDTH, MEM_HEADS * N_MEM)),
            per_batch((MEM_HEADS * N_MEM, MEM_WIDTH)),
            const((D_MODEL, D_MODEL)),
        ],
        out_specs=pl.BlockSpec((1, ts, D_MODEL), lambda b, i: (b, i, 0)),
        out_shape=jax.ShapeDtypeStruct(h.shape, F32),
        scratch_shapes=[
            pltpu.VMEM((ts + SUBLANES, MIX_WIDTH), F32),
            pltpu.VMEM((2, ts // 2, MIX_WIDTH + MEM_WIDTH), F32),
            pltpu.VMEM((MIX_WIDTH // LANES, ts, LANES), F32),
            pltpu.VMEM((MIX_WIDTH // LANES, ts, LANES), F32),
            pltpu.VMEM((SUBLANES, MIX_WIDTH), F32),
        ],
        compiler_params=pltpu.CompilerParams(
            dimension_semantics=("arbitrary", "arbitrary"), vmem_limit_bytes=VMEM_LIMIT_BYTES),
        name="rglru_mix",
    )(h, g, w_in, conv_w, conv_b, wg, b_rg, b_ig, lam, mqg, bd, kbd, vbd, w_out)


def _split3(x):
    p0 = x.astype(BF16)
    r1 = x - p0.astype(F32)
    p1 = r1.astype(BF16)
    p2 = (r1 - p1.astype(F32)).astype(BF16)
    return p0, p1, p2


def _fox_proj_kernel(h_ref, g_ref, w_ref, qg_ref, kg_ref, mqg_ref, bf_ref, bd_ref, tri_ref,
                     q_ref, k_ref, vt_ref, qm_ref, e_ref, base_ref, rel_ref, abs_ref, *, tiles_per_attn):
    t = pl.program_id(1)

    @pl.when(t == 0)
    def _():
        abs_ref[...] = jnp.zeros((1, F_PAD), F32)

    @pl.when(t % tiles_per_attn == 0)
    def _():
        rel_ref[...] = jnp.zeros((1, F_PAD), F32)

    hn = _rms(h_ref[0], g_ref[...]).astype(BF16)
    proj = _dot(hn, w_ref[...])
    w = MIX_WIDTH
    q_ref[0] = (_head_rms(proj[:, :w], qg_ref[...], bd_ref) * (SCORE_SCALE * LOG2E)).astype(BF16)
    k_ref[0] = _head_rms(proj[:, w:2 * w], kg_ref[...], bd_ref).astype(BF16)
    vt = proj[:, 2 * w:3 * w].T.astype(BF16)
    pad_rows = VT_ROWS - HEAD_DIM
    ones_row = jnp.where(lax.broadcasted_iota(jnp.int32, (pad_rows, vt.shape[1]), 0) == 0, 1.0, 0.0)
    for hd in range(MIX_HEADS):
        vt_ref[0, hd, :HEAD_DIM, :] = vt[hd * HEAD_DIM:(hd + 1) * HEAD_DIM, :]
        vt_ref[0, hd, HEAD_DIM:, :] = ones_row.astype(BF16)
    qm = proj[:, 3 * w:3 * w + MEM_WIDTH]
    qm_ref[0] = (_head_rms(qm, mqg_ref[...], bd_ref) * SCORE_SCALE).astype(BF16)

    log_f = -_softplus(-(proj[:, 3 * w + MEM_WIDTH:] + bf_ref[...]))
    tri = tri_ref[...]
    rel = rel_ref[...] + sum(_dot(tri, p) for p in _split3(log_f))
    last = rel[rel.shape[0] - 1:, :]
    lane = lax.broadcasted_iota(jnp.int32, rel.shape, 1)
    e0, e1, e2 = _split3(jnp.where(lane < MIX_HEADS, -LOG2E * rel, 0.0))
    e_ref[0] = (e0.astype(F32) + pltpu.roll(e1.astype(F32), E_STRIDE, 1)
                + pltpu.roll(e2.astype(F32), 2 * E_STRIDE, 1)).astype(BF16)
    base_ref[0, 0] = LOG2E * abs_ref[...]
    rel_ref[...] = last

    @pl.when(t % tiles_per_attn == tiles_per_attn - 1)
    def _():
        abs_ref[...] = abs_ref[...] + last


def _fox_proj(h, g, w, qg, kg, mqg, bf, bd, tri, ts, tq):
    nb, s, _ = h.shape
    const = lambda shape: pl.BlockSpec(shape, lambda b, i: (0,) * len(shape), pipeline_mode=pl.Buffered(1))
    tile = lambda width: pl.BlockSpec((1, ts, width), lambda b, i: (b, i, 0))
    return pl.pallas_call(
        functools.partial(_fox_proj_kernel, tiles_per_attn=tq // ts),
        grid=(nb, s // ts),
        in_specs=[
            tile(D_MODEL),
            const((1, D_MODEL)),
            const((D_MODEL, FOX_COLS)),
            const((1, MIX_WIDTH)),
            const((1, MIX_WIDTH)),
            const((1, MEM_WIDTH)),
            const((1, F_PAD)),
            const((MXU_DIM, MXU_DIM)),
            const((ts, ts)),
        ],
        out_specs=[
            tile(MIX_WIDTH), tile(MIX_WIDTH),
            pl.BlockSpec((1, MIX_HEADS, VT_ROWS, ts), lambda b, i: (b, 0, 0, i)),
            tile(MEM_WIDTH), tile(F_PAD),
            pl.BlockSpec((1, 1, 1, F_PAD), lambda b, i: (b, i, 0, 0)),
        ],
        out_shape=[
            jax.ShapeDtypeStruct((nb, s, MIX_WIDTH), BF16),
            jax.ShapeDtypeStruct((nb, s, MIX_WIDTH), BF16),
            jax.ShapeDtypeStruct((nb, MIX_HEADS, VT_ROWS, s), BF16),
            jax.ShapeDtypeStruct((nb, s, MEM_WIDTH), BF16),
            jax.ShapeDtypeStruct((nb, s, F_PAD), BF16),
            jax.ShapeDtypeStruct((nb, s // ts, 1, F_PAD), F32),
        ],
        scratch_shapes=[pltpu.VMEM((1, F_PAD), F32), pltpu.VMEM((1, F_PAD), F32)],
        compiler_params=pltpu.CompilerParams(
            dimension_semantics=("arbitrary", "arbitrary"), vmem_limit_bytes=VMEM_LIMIT_BYTES),
        name="fox_proj",
    )(h, g, w, qg, kg, mqg, bf, bd, tri)


def _fox_attn_kernel(base_ref, q_ref, k_ref, e_ref, vt_ref, o_ref, qc_ref, m_ref, acc_ref,
                     s_ref, mc_ref, p_ref, al_ref, *, tq, n_tiles):
    b = pl.program_id(0)
    hp = pl.program_id(1)
    lane = lax.broadcasted_iota(jnp.int32, (tq, LANES), 1)

    def values(a, j):
        k_start = pl.multiple_of(j * tq, tq)
        acc_ref[a] = al_ref[a] * acc_ref[a] + _dot(vt_ref[0, a, :, pl.ds(k_start, tq)], p_ref[a])

    def finish_tile(qi):
        last = jnp.where(qi == 0, qi, qi - 1)
        values(0, last)
        values(1, last)
        out_t = jnp.concatenate(
            [acc_ref[a, :HEAD_DIM, :] * (1.0 / acc_ref[a, HEAD_DIM:HEAD_DIM + 1, :]) for a in range(2)],
            axis=0)
        o_ref[0, pl.ds(pl.multiple_of(qi * tq, tq), tq), :] = out_t.T.astype(BF16)

    acc_ref[...] = jnp.zeros(acc_ref.shape, F32)
    al_ref[...] = jnp.zeros(al_ref.shape, F32)
    p_ref[...] = jnp.zeros(p_ref.shape, BF16)

    def query_tile(qi, carry):
        finish_tile(jnp.maximum(qi - 1, 0))
        q_start = pl.multiple_of(qi * tq, tq)
        qf = q_ref[0, pl.ds(q_start, tq), :].astype(F32)
        for a in range(2):
            h = 2 * hp + a
            own = (lane < HEAD_DIM) if a == 0 else (lane >= HEAD_DIM)
            sel = (jnp.bitwise_and(lane, E_STRIDE - 1) == h) & (lane < 3 * E_STRIDE)
            qc_ref[a] = jnp.concatenate(
                [jnp.where(own, qf, 0.0), jnp.where(sel, 1.0, 0.0)], axis=-1).astype(BF16)
        m_ref[...] = jnp.full(m_ref.shape, -jnp.inf, F32)
        acc_ref[...] = jnp.zeros(acc_ref.shape, F32)

        def scores(a, j, slot, masked):
            k_start = pl.multiple_of(j * tq, tq)
            kc = jnp.concatenate(
                [k_ref[0, pl.ds(k_start, tq), :], e_ref[0, pl.ds(k_start, tq), :]], axis=-1)
            raw = _dot_nt(kc, qc_ref[a])
            if masked:
                key = lax.broadcasted_iota(jnp.int32, (tq, tq), 0)
                qry = lax.broadcasted_iota(jnp.int32, (tq, tq), 1)
                raw = jnp.where(key <= qry, raw, -jnp.inf)
            s_ref[a, slot] = raw
            mc_ref[a, slot] = jnp.max(raw, axis=0, keepdims=True)

        def softmax(a, j, slot):
            h = 2 * hp + a
            delta = (base_ref[(b * n_tiles + qi) * E_STRIDE + h]
                     - base_ref[(b * n_tiles + j) * E_STRIDE + h])
            m_prev = m_ref[a]
            m_new = jnp.maximum(m_prev, mc_ref[a, slot] + delta)
            al_ref[a] = jnp.exp2(m_prev - m_new)
            p_ref[a] = jnp.exp2(s_ref[a, slot] - (m_new - delta)).astype(BF16)
            m_ref[a] = m_new

        def round_(t, slot, issue_scores):
            prev = jnp.where(t == 1, qi, t - 2)
            values(0, prev)
            values(1, prev)
            if issue_scores:
                scores(0, t, 1 - slot, False)
                scores(1, t, 1 - slot, False)
            softmax(0, t - 1, slot)
            softmax(1, t - 1, slot)

        scores(0, qi, 0, True)
        scores(1, qi, 0, True)
        scores(0, 0, 1, False)
        scores(1, 0, 1, False)
        softmax(0, qi, 0)
        softmax(1, qi, 0)

        def round_pair(i, c):
            round_(2 * i + 1, 1, True)
            round_(2 * i + 2, 0, True)
            return c

        lax.fori_loop(0, (qi - 1) // 2, round_pair, 0)

        @pl.when(qi % 2 == 1)
        def _():
            round_(qi, 1, False)

        @pl.when((qi % 2 == 0) & (qi > 0))
        def _():
            round_(qi - 1, 1, True)
            round_(qi, 0, False)

        return carry

    lax.fori_loop(0, n_tiles, query_tile, 0)
    finish_tile(n_tiles - 1)


def _fox_attention(base, q, k, e, vt, tq):
    nb, s, _ = q.shape
    seq_block = lambda width: pl.BlockSpec((1, s, width), lambda b, p: (b, 0, p))
    return pl.pallas_call(
        functools.partial(_fox_attn_kernel, tq=tq, n_tiles=s // tq),
        grid=(nb, MIX_WIDTH // LANES),
        in_specs=[
            pl.BlockSpec(memory_space=pltpu.SMEM),
            seq_block(LANES),
            seq_block(LANES),
            pl.BlockSpec((1, s, F_PAD), lambda b, p: (b, 0, 0)),
            pl.BlockSpec((1, 2, VT_ROWS, s), lambda b, p: (b, p, 0, 0)),
        ],
        out_specs=seq_block(LANES),
        out_shape=jax.ShapeDtypeStruct((nb, s, MIX_WIDTH), BF16),
        scratch_shapes=[
            pltpu.VMEM((2, tq, 2 * LANES), BF16),
            pltpu.VMEM((2, 1, tq), F32),
            pltpu.VMEM((2, VT_ROWS, tq), F32),
            pltpu.VMEM((2, 2, tq, tq), F32),
            pltpu.VMEM((2, 2, 1, tq), F32),
            pltpu.VMEM((2, tq, tq), BF16),
            pltpu.VMEM((2, 1, tq), F32),
        ],
        compiler_params=pltpu.CompilerParams(
            dimension_semantics=("arbitrary", "arbitrary"), vmem_limit_bytes=VMEM_LIMIT_BYTES),
        name="fox_attention",
    )(base, q, k, e, vt)


def _fox_out_kernel(h_ref, tok_ref, qm_ref, kbd_ref, vbd_ref, wout_ref, o_ref):
    half = h_ref.shape[1] // 2
    rows = [slice(k * half, (k + 1) * half) for k in range(2)]
    s = [_dot(qm_ref[0, r, :], kbd_ref[0]) for r in rows]
    cross = [None, None]
    cross[0] = _dot(_cross_probs(s[0]), vbd_ref[0])
    p1 = _cross_probs(s[1])
    o_ref[0, rows[0], :] = _mix_out(h_ref[0, rows[0], :], tok_ref[0, rows[0], :], cross[0], wout_ref)
    cross[1] = _dot(p1, vbd_ref[0])
    o_ref[0, rows[1], :] = _mix_out(h_ref[0, rows[1], :], tok_ref[0, rows[1], :], cross[1], wout_ref)


def _fox_out(h, tok, qm, kbd, vbd, w_out, ts):
    nb, s, _ = h.shape
    tile = lambda width: pl.BlockSpec((1, ts, width), lambda b, i: (b, i, 0))
    per_batch = lambda shape: pl.BlockSpec((1,) + shape, lambda b, i: (b, 0, 0))
    return pl.pallas_call(
        _fox_out_kernel,
        grid=(nb, s // ts),
        in_specs=[
            tile(D_MODEL), tile(MIX_WIDTH), tile(MEM_WIDTH),
            per_batch((MEM_WIDTH, MEM_HEADS * N_MEM)),
            per_batch((MEM_HEADS * N_MEM, MEM_WIDTH)),
            pl.BlockSpec((D_MODEL, D_MODEL), lambda b, i: (0, 0), pipeline_mode=pl.Buffered(1)),
        ],
        out_specs=tile(D_MODEL),
        out_shape=jax.ShapeDtypeStruct(h.shape, F32),
        compiler_params=pltpu.CompilerParams(
            dimension_semantics=("arbitrary", "arbitrary"), vmem_limit_bytes=VMEM_LIMIT_BYTES),
        name="fox_out",
    )(h, tok, qm, kbd, vbd, w_out)


def _row(v):
    return v.reshape(1, -1).astype(F32)


def _tile_heads(g, heads):
    return jnp.tile(g.astype(F32), heads).reshape(1, -1)


def _gate_weights(w_rg, w_ig):
    def block_diag(w):
        groups = w.reshape(MIX_HEADS // HEADS_PER_MXU_TILE, HEADS_PER_MXU_TILE, HEAD_DIM, HEAD_DIM)
        eye = jnp.eye(HEADS_PER_MXU_TILE, dtype=w.dtype)
        bd = jnp.einsum('ghij,hk->ghikj', groups, eye)
        return bd.reshape(-1, MXU_DIM, MXU_DIM)
    return jnp.concatenate([block_diag(w_rg), block_diag(w_ig)], axis=-1).astype(BF16)


def _fox_weights(w):
    qkv = w[:, :3 * MIX_WIDTH]
    f = w[:, 3 * MIX_WIDTH:3 * MIX_WIDTH + MIX_HEADS]
    qm = w[:, 3 * MIX_WIDTH + MIX_HEADS:]
    f = jnp.pad(f, ((0, 0), (0, F_PAD - MIX_HEADS)))
    return jnp.concatenate([qkv, qm, f], axis=-1).astype(BF16)


def kernel(x, mem, mem_norm_g, mem_w_kv, mem_k_norm_g, ffn1_norm_g, ffn1_w_in, ffn1_w_out, mix_norm_g, mix_w_out, memq_norm_g, ffn2_norm_g, ffn2_w_in, ffn2_w_out, lru_w_in, lru_conv_w, lru_conv_b, lru_w_rg, lru_b_rg, lru_w_ig, lru_b_ig, lru_lambda, fox_w_in, fox_b_f, fox_q_norm_g, fox_k_norm_g):
    nb, s, d = x.shape
    assert d == D_MODEL and mem.shape[1:] == (N_MEM, D_MODEL)
    ts = min(ROW_TILE, s)
    tm = min(MIX_ROW_TILE, s)
    tq = min(ATTN_TILE, s)
    tf = min(FFN_ROW_TILE, nb * s)
    assert s % ts == 0 and s % tm == 0 and s % tq == 0 and tq % ts == 0 and (nb * s) % tf == 0

    head_id = jnp.arange(MXU_DIM) // HEAD_DIM
    bd = (head_id[:, None] == head_id[None, :]).astype(BF16)
    tri = (jnp.arange(ts)[:, None] >= jnp.arange(ts)[None, :]).astype(BF16)

    kbd, vbd = _memory_kv(mem, _row(mem_norm_g), mem_w_kv.astype(BF16),
                          _tile_heads(mem_k_norm_g, MEM_HEADS), bd)

    ffn_weights = {1: (ffn1_norm_g, ffn1_w_in.astype(BF16), ffn1_w_out.astype(BF16)),
                   2: (ffn2_norm_g, ffn2_w_in.astype(BF16), ffn2_w_out.astype(BF16))}

    def ffn(h, which, layer):
        g, w_in, w_out = ffn_weights[which]
        out = _ffn(h.reshape(nb * s, d), _row(g[layer]), w_in, w_out, layer, tf)
        return out.reshape(nb, s, d)

    h = x
    h = ffn(h, 1, 0)
    h = _lru_mix(h, _row(mix_norm_g[0]), lru_w_in[0].astype(BF16), lru_conv_w[0].astype(F32),
                 _row(lru_conv_b[0]), _gate_weights(lru_w_rg[0], lru_w_ig[0]), _row(lru_b_rg[0]),
                 _row(lru_b_ig[0]), _row(lru_lambda[0]), _tile_heads(memq_norm_g[0], MEM_HEADS),
                 bd, kbd, vbd, mix_w_out[0].astype(BF16), tm)
    h = ffn(h, 2, 0)
    h = ffn(h, 1, 1)
    bf = jnp.pad(fox_b_f[0].astype(F32), (0, F_PAD - MIX_HEADS)).reshape(1, F_PAD)
    q, k, vt, qm, e, base = _fox_proj(h, _row(mix_norm_g[1]), _fox_weights(fox_w_in[0]),
                                      _tile_heads(fox_q_norm_g[0], MIX_HEADS),
                                      _tile_heads(fox_k_norm_g[0], MIX_HEADS),
                                      _tile_heads(memq_norm_g[1], MEM_HEADS), bf, bd, tri, ts, tq)
    base = base[:, ::tq // ts, 0, :E_STRIDE].reshape(-1)
    tok = _fox_attention(base, q, k, e, vt, tq)
    h = _fox_out(h, tok, qm, kbd, vbd, mix_w_out[1].astype(BF16), tm)
    h = ffn(h, 2, 1)
    return h
```
